```python
import jax, jax.numpy as jnp
from jax import lax
import numpy as np

D_MODEL = 2048
BATCH = 16
SEQ = 2048
DEPTH = 1

HEAD_DIM = 64
N_Q_HEADS = 16
N_KV_HEADS = 2
ATTN_WIDTH = N_Q_HEADS * HEAD_DIM
KV_WIDTH = N_KV_HEADS * HEAD_DIM
CONV_WIDTH = D_MODEL - ATTN_WIDTH
MIX_WIDTH = ATTN_WIDTH + CONV_WIDTH
IN_WIDTH = ATTN_WIDTH + 2 * KV_WIDTH + 2 * CONV_WIDTH
WINDOW = 128
BLOCK = 128
CONV_KERNEL = 31
ROPE_THETA = 10000.0
D_FF = -(-8 * D_MODEL // (3 * 256)) * 256
LN_EPS = 1e-5
DEEPNORM_ALPHA = (2 * DEPTH) ** 0.25
DEEPNORM_BETA = (8 * DEPTH) ** -0.25

kernel_name = "hymba_swa_sink_conformer_conv_deepnorm"


def layer_norm(x, g, b):
    xf = x.astype(jnp.float32)
    mu = jnp.mean(xf, axis=-1, keepdims=True)
    var = jnp.mean(jnp.square(xf - mu), axis=-1, keepdims=True)
    y = (xf - mu) * lax.rsqrt(var + LN_EPS)
    return (y * g.astype(jnp.float32) + b.astype(jnp.float32)).astype(x.dtype)


def rope(x, positions):
    half = HEAD_DIM // 2
    inv_freq = 1.0 / (ROPE_THETA ** (jnp.arange(half, dtype=jnp.float32) * 2.0 / HEAD_DIM))
    ang = positions.astype(jnp.float32)[:, :, None] * inv_freq
    cos = jnp.cos(ang)[:, :, None, :]
    sin = jnp.sin(ang)[:, :, None, :]
    xf = x.astype(jnp.float32)
    x1, x2 = xf[..., :half], xf[..., half:]
    out = jnp.concatenate([x1 * cos - x2 * sin, x2 * cos + x1 * sin], axis=-1)
    return out.astype(x.dtype)


def sliding_window_sink_attention(q, k, v, sinks):
    b, s, _, _ = q.shape
    nb = s // BLOCK
    g = N_Q_HEADS // N_KV_HEADS
    qb = q.reshape(b, nb, BLOCK, N_KV_HEADS, g, HEAD_DIM)
    kb = k.reshape(b, nb, BLOCK, N_KV_HEADS, HEAD_DIM)
    vb = v.reshape(b, nb, BLOCK, N_KV_HEADS, HEAD_DIM)
    pad = ((0, 0), (1, 0), (0, 0), (0, 0), (0, 0))
    kk = jnp.concatenate([jnp.pad(kb, pad)[:, :-1], kb], axis=2)
    vv = jnp.concatenate([jnp.pad(vb, pad)[:, :-1], vb], axis=2)
    scores = jnp.einsum('bnqkgd,bnskd->bnkgqs', qb, kk).astype(jnp.float32) * (HEAD_DIM ** -0.5)
    qi = jnp.arange(BLOCK)[:, None]
    si = jnp.arange(2 * BLOCK)[None, :]
    diff = qi + BLOCK - si
    band = (diff >= 0) & (diff < WINDOW)
    key_pos = jnp.arange(nb)[:, None, None] * BLOCK - BLOCK + si[None]
    mask = band[None] & (key_pos >= 0)
    scores = jnp.where(mask[None, :, None, None], scores, -jnp.inf)
    sink = sinks.astype(jnp.float32).reshape(N_KV_HEADS, g)[None, None, :, :, None, None]
    m = jnp.maximum(jnp.max(scores, axis=-1, keepdims=True), sink)
    p = jnp.exp(scores - m)
    denom = jnp.sum(p, axis=-1, keepdims=True) + jnp.exp(sink - m)
    probs = (p / denom).astype(v.dtype)
    out = jnp.einsum('bnkgqs,bnskd->bnqkgd', probs, vv)
    return out.reshape(b, s, ATTN_WIDTH)


def conformer_conv(a, gate, w_dw, b_dw, ln_g, ln_b, w_pw2, b_pw2):
    u = a * jax.nn.sigmoid(gate)
    u = lax.conv_general_dilated(u, w_dw.astype(u.dtype), window_strides=(1,),
                                 padding=[(CONV_KERNEL - 1, 0)],
                                 dimension_numbers=('NWC', 'WIO', 'NWC'),
                                 feature_group_count=CONV_WIDTH) + b_dw
    u = jax.nn.silu(layer_norm(u, ln_g, ln_b))
    return u @ w_pw2 + b_pw2


def setup_inputs(seed: int = 0) -> dict:
    key = jax.random.key(seed)
    ks = jax.random.split(key, 24)
    f32 = jnp.float32
    nrm = lambda k, shape, scale: jax.random.normal(k, shape, f32) * scale
    L = DEPTH
    x = jax.random.normal(ks[0], (BATCH, SEQ, D_MODEL), f32)
    offsets = jax.random.randint(ks[1], (BATCH, 1), 0, 4096, dtype=jnp.int32)
    positions = offsets + jnp.arange(SEQ, dtype=jnp.int32)[None, :]
    return {
        "x": x,
        "positions": positions,
        "w_in": nrm(ks[2], (L, D_MODEL, IN_WIDTH), D_MODEL ** -0.5),
        "b_in": nrm(ks[3], (L, IN_WIDTH), 0.02),
        "sinks": nrm(ks[4], (L, N_Q_HEADS), 0.5),
        "w_dw": nrm(ks[5], (L, CONV_KERNEL, 1, CONV_WIDTH), CONV_KERNEL ** -0.5),
        "b_dw": nrm(ks[6], (L, CONV_WIDTH), 0.02),
        "conv_ln_g": 1.0 + nrm(ks[7], (L, CONV_WIDTH), 0.02),
        "conv_ln_b": nrm(ks[8], (L, CONV_WIDTH), 0.02),
        "w_pw2": nrm(ks[9], (L, CONV_WIDTH, CONV_WIDTH), CONV_WIDTH ** -0.5),
        "b_pw2": nrm(ks[10], (L, CONV_WIDTH), 0.02),
        "w_out": nrm(ks[11], (L, MIX_WIDTH, D_MODEL), MIX_WIDTH ** -0.5 * DEEPNORM_BETA),
        "b_out": nrm(ks[12], (L, D_MODEL), 0.02),
        "ln1_g": 1.0 + nrm(ks[13], (L, D_MODEL), 0.02),
        "ln1_b": nrm(ks[14], (L, D_MODEL), 0.02),
        "w_gate": nrm(ks[15], (L, D_MODEL, D_FF), D_MODEL ** -0.5),
        "w_up": nrm(ks[16], (L, D_MODEL, D_FF), D_MODEL ** -0.5),
        "w_down": nrm(ks[17], (L, D_FF, D_MODEL), D_FF ** -0.5 * DEEPNORM_BETA),
        "ln2_g": 1.0 + nrm(ks[18], (L, D_MODEL), 0.02),
        "ln2_b": nrm(ks[19], (L, D_MODEL), 0.02),
    }


def reference(x, positions, w_in, b_in, sinks, w_dw, b_dw, conv_ln_g, conv_ln_b, w_pw2, b_pw2,
              w_out, b_out, ln1_g, ln1_b, w_gate, w_up, w_down, ln2_g, ln2_b):
    b, s, _ = x.shape
    o_k = ATTN_WIDTH
    o_v = o_k + KV_WIDTH
    o_a = o_v + KV_WIDTH
    o_g = o_a + CONV_WIDTH
    for l in range(DEPTH):
        h = x @ w_in[l] + b_in[l]
        q = rope(h[..., :o_k].reshape(b, s, N_Q_HEADS, HEAD_DIM), positions)
        k = rope(h[..., o_k:o_v].reshape(b, s, N_KV_HEADS, HEAD_DIM), positions)
        v = h[..., o_v:o_a].reshape(b, s, N_KV_HEADS, HEAD_DIM)
        attn = sliding_window_sink_attention(q, k, v, sinks[l])
        conv = conformer_conv(h[..., o_a:o_g], h[..., o_g:], w_dw[l], b_dw[l],
                              conv_ln_g[l], conv_ln_b[l], w_pw2[l], b_pw2[l])
        mix = jnp.concatenate([attn, conv], axis=-1) @ w_out[l] + b_out[l]
        x = layer_norm(DEEPNORM_ALPHA * x + mix, ln1_g[l], ln1_b[l])
        ffn = (jax.nn.silu(x @ w_gate[l]) * (x @ w_up[l])) @ w_down[l]
        x = layer_norm(DEEPNORM_ALPHA * x + ffn, ln2_g[l], ln2_b[l])
    return x
```

```python
import functools

import jax
import jax.numpy as jnp
from jax import lax
from jax.experimental import pallas as pl
from jax.experimental.pallas import tpu as pltpu

HEAD_DIM = 64
N_Q_HEADS = 16
N_KV_HEADS = 2
WINDOW = 128
BLOCK = 128
CONV_KERNEL = 31
ROPE_THETA = 10000.0
LN_EPS = 1e-5

LANES = 128
SUBLANES = 8
CONV_HALO = 32
VMEM_LIMIT_BYTES = 56 * 1024 * 1024

F32 = jnp.float32
BF16 = jnp.bfloat16


def _resident(shape):
    return pl.BlockSpec(shape, lambda *_: (0,) * len(shape), pipeline_mode=pl.Buffered(1))


def _layer_norm(y, g, b):
    mu = jnp.mean(y, axis=-1, keepdims=True)
    d = y - mu
    var = jnp.mean(d * d, axis=-1, keepdims=True)
    return d * lax.rsqrt(var + LN_EPS) * g + b


def _in_proj_kernel(x_ref, pos_ref, invf_ref, w_ref, b_ref, q_ref, kd_ref, vd_ref, u_ref, *, attn_w, kv_w, conv_w):
    xb = x_ref[...].astype(BF16)
    lane = lax.broadcasted_iota(jnp.int32, (1, LANES), 1)
    first_half = (lane % HEAD_DIM) < (HEAD_DIM // 2)
    low_head = lane < HEAD_DIM
    ang = pos_ref[...].astype(F32) * invf_ref[...]
    cos = jnp.cos(ang)
    sin = jnp.sin(ang)
    sin_signed = jnp.where(first_half, -sin, sin)

    def rope(h):
        rot = jnp.where(first_half, pltpu.roll(h, LANES - HEAD_DIM // 2, 1), pltpu.roll(h, HEAD_DIM // 2, 1))
        return h * cos + rot * sin_signed

    def proj(c0, width):
        return jnp.dot(xb, w_ref[:, c0:c0 + width], preferred_element_type=F32) + b_ref[:, c0:c0 + width]

    chunk = 4 * LANES
    for c0 in range(0, attn_w, chunk):
        h = proj(c0, chunk)
        for s in range(0, chunk, LANES):
            q_ref[:, c0 + s:c0 + s + LANES] = (rope(h[:, s:s + LANES]) * (HEAD_DIM ** -0.5)).astype(BF16)

    hkv = proj(attn_w, 2 * kv_w)
    k = rope(hkv[:, :kv_w])
    v = hkv[:, kv_w:]
    for src, dst in ((k, kd_ref), (v, vd_ref)):
        swapped = pltpu.roll(src, HEAD_DIM, 1)
        dst[:, :LANES] = jnp.where(low_head, src, swapped).astype(BF16)
        dst[:, LANES:] = jnp.where(low_head, swapped, src).astype(BF16)

    o_a = attn_w + 2 * kv_w
    o_g = o_a + conv_w
    for c0 in range(0, conv_w, chunk):
        a = proj(o_a + c0, chunk)
        g = proj(o_g + c0, chunk)
        u_ref[:, c0:c0 + chunk] = a * jax.nn.sigmoid(g)


def _in_proj(x2, pos2, invf, w, b, *, tm):
    t, d = x2.shape
    attn_w = N_Q_HEADS * HEAD_DIM
    kv_w = N_KV_HEADS * HEAD_DIM
    conv_w = (w.shape[1] - attn_w - 2 * kv_w) // 2
    assert kv_w == LANES and attn_w % (4 * LANES) == 0 and conv_w % (4 * LANES) == 0
    row = lambda width: pl.BlockSpec((tm, width), lambda i: (i, 0))
    return pl.pallas_call(
        functools.partial(_in_proj_kernel, attn_w=attn_w, kv_w=kv_w, conv_w=conv_w),
        grid=(t // tm,),
        in_specs=[row(d), row(1), _resident(invf.shape), _resident(w.shape), _resident(b.shape)],
        out_specs=[row(attn_w), row(2 * kv_w), row(2 * kv_w), row(conv_w)],
        out_shape=[jax.ShapeDtypeStruct((t, attn_w), BF16), jax.ShapeDtypeStruct((t, 2 * kv_w), BF16),
                   jax.ShapeDtypeStruct((t, 2 * kv_w), BF16), jax.ShapeDtypeStruct((t, conv_w), F32)],
        compiler_params=pltpu.CompilerParams(dimension_semantics=("arbitrary",), vmem_limit_bytes=VMEM_LIMIT_BYTES),
        name="in_proj",
    )(x2, pos2, invf, w, b)


def _attn_kernel(sinks_ref, q_ref, kc_ref, kp_ref, vc_ref, vp_ref, o_ref, kfull_ref, vfull_ref, *, tq):
    i = pl.program_id(1)
    kfull_ref[:BLOCK, :] = kp_ref[...]
    kfull_ref[BLOCK:, :] = kc_ref[...]
    vfull_ref[:BLOCK, :] = vp_ref[...]
    vfull_ref[BLOCK:, :] = vc_ref[...]

    lane = lax.broadcasted_iota(jnp.int32, (1, LANES), 1)
    low_head = lane < HEAD_DIM
    qi = lax.broadcasted_iota(jnp.int32, (BLOCK, 2 * BLOCK), 0)
    si = lax.broadcasted_iota(jnp.int32, (BLOCK, 2 * BLOCK), 1)
    diff = qi + BLOCK - si
    band = (diff >= 0) & (diff < WINDOW)
    heads_per_kv = N_Q_HEADS // N_KV_HEADS
    chunks_per_kv = heads_per_kv * HEAD_DIM // LANES
    zero = jnp.zeros((), BF16)

    def block_body(j, carry):
        r0 = pl.multiple_of(j * BLOCK, BLOCK)
        first_key = jnp.where((i > 0) | (j > 0), 0, BLOCK)
        valid = band & (si >= first_key)
        for g in range(N_KV_HEADS):
            kk = kfull_ref[pl.ds(r0, 2 * BLOCK), g * LANES:(g + 1) * LANES]
            vv = vfull_ref[pl.ds(r0, 2 * BLOCK), g * LANES:(g + 1) * LANES]
            k2 = jnp.concatenate([jnp.where(low_head, kk, zero), jnp.where(low_head, zero, kk)], axis=0)
            v2 = jnp.concatenate([jnp.where(low_head, vv, zero), jnp.where(low_head, zero, vv)], axis=0)
            for c in range(chunks_per_kv):
                col = (g * chunks_per_kv + c) * LANES
                qc = q_ref[pl.ds(r0, BLOCK), col:col + LANES]
                s2 = lax.dot_general(qc, k2, (((1,), (1,)), ((), ())), preferred_element_type=F32)
                probs = []
                for hh in range(2):
                    sink = sinks_ref[g * heads_per_kv + 2 * c + hh]
                    s = jnp.where(valid, s2[:, hh * 2 * BLOCK:(hh + 1) * 2 * BLOCK], -jnp.inf)
                    m = jnp.maximum(jnp.max(s, axis=-1, keepdims=True), sink)
                    p = jnp.exp(s - m)
                    denom = jnp.sum(p, axis=-1, keepdims=True) + jnp.exp(sink - m)
                    probs.append((p * (1.0 / denom)).astype(BF16))
                p2 = jnp.concatenate(probs, axis=1)
                o = jnp.dot(p2, v2, preferred_element_type=F32)
                o_ref[pl.ds(r0, BLOCK), col:col + LANES] = o.astype(BF16)
        return carry

    lax.fori_loop(0, tq // BLOCK, block_body, 0)


def _attention(sinks, q, kd, vd, *, batch, seq, tq):
    t, attn_w = q.shape
    kvw = kd.shape[1]
    tiles = seq // tq
    blocks_per_tile = tq // BLOCK
    cur = lambda width: pl.BlockSpec((tq, width), lambda b, i: (b * tiles + i, 0))
    prev = pl.BlockSpec((BLOCK, kvw),
                        lambda b, i: (b * (seq // BLOCK) + jnp.maximum(i * blocks_per_tile - 1, 0), 0))
    return pl.pallas_call(
        functools.partial(_attn_kernel, tq=tq),
        grid=(batch, tiles),
        in_specs=[pl.BlockSpec(memory_space=pltpu.SMEM), cur(attn_w), cur(kvw), prev, cur(kvw), prev],
        out_specs=cur(attn_w),
        out_shape=jax.ShapeDtypeStruct((t, attn_w), BF16),
        scratch_shapes=[pltpu.VMEM((tq + BLOCK, kvw), BF16), pltpu.VMEM((tq + BLOCK, kvw), BF16)],
        compiler_params=pltpu.CompilerParams(dimension_semantics=("arbitrary", "arbitrary"),
                                             vmem_limit_bytes=VMEM_LIMIT_BYTES),
        name="swa_attention",
    )(sinks, q, kd, kd, vd, vd)


def _conv_kernel(uc_ref, up_ref, wdw_ref, bdw_ref, lng_ref, lnb_ref, wpw_ref, bpw_ref, o_ref, ext_ref, y_ref, *, tc, rows):
    i = pl.program_id(1)
    halo = up_ref[...]
    ext_ref[:CONV_HALO, :] = jnp.where(i > 0, halo, jnp.zeros_like(halo))
    ext_ref[CONV_HALO:, :] = uc_ref[...]
    width = uc_ref.shape[1]
    shift0 = CONV_HALO - (CONV_KERNEL - 1)

    for c0 in range(0, width, LANES):
        w = wdw_ref[:, c0:c0 + LANES]
        bias = bdw_ref[:, c0:c0 + LANES]

        def row_body(r, carry, c0=c0, w=w, bias=bias):
            base = pl.multiple_of(r * rows, rows)
            win = ext_ref[pl.ds(base, rows + CONV_HALO), c0:c0 + LANES]
            acc = jnp.broadcast_to(bias, (rows, LANES))
            for b in range(SUBLANES):
                wb = win if b == 0 else pltpu.roll(win, rows + CONV_HALO - b, 0)
                for a in range(CONV_HALO // SUBLANES + 1):
                    j = SUBLANES * a + b - shift0
                    if 0 <= j < CONV_KERNEL:
                        acc = acc + wb[SUBLANES * a:SUBLANES * a + rows, :] * w[j:j + 1, :]
            y_ref[pl.ds(base, rows), c0:c0 + LANES] = acc
            return carry

        lax.fori_loop(0, tc // rows, row_body, 0)

    y = _layer_norm(y_ref[...], lng_ref[...], lnb_ref[...])
    act = jax.nn.silu(y).astype(BF16)
    o_ref[...] = (jnp.dot(act, wpw_ref[...], preferred_element_type=F32) + bpw_ref[...]).astype(BF16)


def _conv_branch(u, wdw, bdw, lng, lnb, wpw, bpw, *, batch, seq, tc, rows=64):
    t, width = u.shape
    tiles = seq // tc
    cur = pl.BlockSpec((tc, width), lambda b, i: (b * tiles + i, 0))
    prev = pl.BlockSpec((CONV_HALO, width),
                        lambda b, i: (b * (seq // CONV_HALO) + jnp.maximum(i * (tc // CONV_HALO) - 1, 0), 0))
    return pl.pallas_call(
        functools.partial(_conv_kernel, tc=tc, rows=rows),
        grid=(batch, tiles),
        in_specs=[cur, prev, _resident(wdw.shape), _resident(bdw.shape), _resident(lng.shape), _resident(lnb.shape),
                  _resident(wpw.shape), _resident(bpw.shape)],
        out_specs=cur,
        out_shape=jax.ShapeDtypeStruct((t, width), BF16),
        scratch_shapes=[pltpu.VMEM((tc + CONV_HALO, width), F32), pltpu.VMEM((tc, width), F32)],
        compiler_params=pltpu.CompilerParams(dimension_semantics=("arbitrary", "arbitrary"),
                                             vmem_limit_bytes=VMEM_LIMIT_BYTES),
        name="conformer_conv",
    )(u, u, wdw, bdw, lng, lnb, wpw, bpw)


def _out_proj_kernel(a_ref, c_ref, x_ref, w_ref, b_ref, g_ref, beta_ref, o_ref, *, alpha):
    attn_w = a_ref.shape[1]
    mix = jnp.dot(a_ref[...], w_ref[:attn_w, :], preferred_element_type=F32)
    mix = mix + jnp.dot(c_ref[...], w_ref[attn_w:, :], preferred_element_type=F32)
    y = alpha * x_ref[...] + (mix + b_ref[...])
    o_ref[...] = _layer_norm(y, g_ref[...], beta_ref[...])


def _out_proj(attn, conv, x2, w, b, g, beta, *, alpha, tm):
    t, d = x2.shape
    row = lambda width: pl.BlockSpec((tm, width), lambda i: (i, 0))
    return pl.pallas_call(
        functools.partial(_out_proj_kernel, alpha=alpha),
        grid=(t // tm,),
        in_specs=[row(attn.shape[1]), row(conv.shape[1]), row(d), _resident(w.shape), _resident(b.shape),
                  _resident(g.shape), _resident(beta.shape)],
        out_specs=row(d),
        out_shape=jax.ShapeDtypeStruct((t, d), F32),
        compiler_params=pltpu.CompilerParams(dimension_semantics=("arbitrary",), vmem_limit_bytes=VMEM_LIMIT_BYTES),
        name="out_proj_ln",
    )(attn, conv, x2, w, b, g, beta)


def _ffn_kernel(x_ref, wg_ref, wu_ref, wd_ref, g_ref, beta_ref, o_ref, xb_ref, *, alpha):
    f = pl.program_id(1)

    @pl.when(f == 0)
    def _():
        xb_ref[...] = x_ref[...].astype(BF16)

    xb = xb_ref[...]
    gate = jnp.dot(xb, wg_ref[0], preferred_element_type=F32)
    up = jnp.dot(xb, wu_ref[0], preferred_element_type=F32)
    hidden = (jax.nn.silu(gate) * up).astype(BF16)

    @pl.when(f == 0)
    def _():
        o_ref[...] = alpha * x_ref[...]

    half = o_ref.shape[1] // 2
    for n0 in (0, half):
        o_ref[:, n0:n0 + half] += jnp.dot(hidden, wd_ref[:, n0:n0 + half], preferred_element_type=F32)

    @pl.when(f == pl.num_programs(1) - 1)
    def _():
        o_ref[...] = _layer_norm(o_ref[...], g_ref[...], beta_ref[...])


def _ffn(x1, wg, wu, wd, g, beta, *, alpha, tm):
    t, d = x1.shape
    nf, _, tf = wg.shape
    return pl.pallas_call(
        functools.partial(_ffn_kernel, alpha=alpha),
        grid=(t // tm, nf),
        in_specs=[pl.BlockSpec((tm, d), lambda i, f: (i, 0)),
                  pl.BlockSpec((1, d, tf), lambda i, f: (f, 0, 0)),
                  pl.BlockSpec((1, d, tf), lambda i, f: (f, 0, 0)),
                  pl.BlockSpec((tf, d), lambda i, f: (f, 0)),
                  _resident(g.shape), _resident(beta.shape)],
        out_specs=pl.BlockSpec((tm, d), lambda i, f: (i, 0)),
        out_shape=jax.ShapeDtypeStruct((t, d), F32),
        scratch_shapes=[pltpu.VMEM((tm, d), BF16)],
        compiler_params=pltpu.CompilerParams(dimension_semantics=("arbitrary", "arbitrary"),
                                             vmem_limit_bytes=VMEM_LIMIT_BYTES),
        name="swiglu_ffn_ln",
    )(x1, wg, wu, wd, g, beta)


def _chunk_columns(w, tf):
    d, f = w.shape
    return w.reshape(d, f // tf, tf).transpose(1, 0, 2)


def kernel(x, positions, w_in, b_in, sinks, w_dw, b_dw, conv_ln_g, conv_ln_b, w_pw2, b_pw2, w_out, b_out, ln1_g,
           ln1_b, w_gate, w_up, w_down, ln2_g, ln2_b):
    batch, seq, d = x.shape
    depth = w_in.shape[0]
    alpha = (2 * depth) ** 0.25
    t = batch * seq
    tf = 512
    row = lambda a: a.reshape(1, -1).astype(F32)

    half = HEAD_DIM // 2
    inv_freq = 1.0 / (ROPE_THETA ** (jnp.arange(half, dtype=F32) * 2.0 / HEAD_DIM))
    invf = jnp.tile(inv_freq, LANES // half).reshape(1, LANES)
    pos2 = positions.reshape(t, 1)

    x2 = x.reshape(t, d)
    for l in range(depth):
        q, kd, vd, u = _in_proj(x2, pos2, invf, w_in[l].astype(BF16), row(b_in[l]), tm=512)
        attn = _attention(sinks[l].astype(F32), q, kd, vd, batch=batch, seq=seq, tq=512)
        wdw = w_dw[l].reshape(CONV_KERNEL, -1).astype(F32)
        conv = _conv_branch(u, wdw, row(b_dw[l]), row(conv_ln_g[l]), row(conv_ln_b[l]), w_pw2[l].astype(BF16),
                            row(b_pw2[l]), batch=batch, seq=seq, tc=512)
        x2 = _out_proj(attn, conv, x2, w_out[l].astype(BF16), row(b_out[l]), row(ln1_g[l]), row(ln1_b[l]),
                       alpha=alpha, tm=512)
        x2 = _ffn(x2, _chunk_columns(w_gate[l].astype(BF16), tf), _chunk_columns(w_up[l].astype(BF16), tf),
                  w_down[l].astype(BF16), row(ln2_g[l]), row(ln2_b[l]), alpha=alpha, tm=512)
    return x2.reshape(batch, seq, d)
```

```python
import functools

import jax
import jax.numpy as jnp
from jax import lax
from jax.experimental import pallas as pl
from jax.experimental.pallas import tpu as pltpu

HEAD_DIM = 64
N_Q_HEADS = 16
N_KV_HEADS = 2
WINDOW = 128
BLOCK = 128
CONV_KERNEL = 31
ROPE_THETA = 10000.0
LN_EPS = 1e-5

LANES = 128
SUBLANES = 8
CONV_HALO = 32
VMEM_LIMIT_BYTES = 60 * 1024 * 1024

TOKEN_TILE = 1024
ATTN_TILE = 512
CONV_TILE = 512
FFN_CHUNK = 512

F32 = jnp.float32
BF16 = jnp.bfloat16


def _resident(shape):
    return pl.BlockSpec(shape, lambda *_: (0,) * len(shape), pipeline_mode=pl.Buffered(1))


def _layer_norm(y, g, b):
    mu = jnp.mean(y, axis=-1, keepdims=True)
    d = y - mu
    var = jnp.mean(d * d, axis=-1, keepdims=True)
    return d * lax.rsqrt(var + LN_EPS) * g + b


def _in_proj_kernel(x_ref, pos_ref, invf_ref, w_ref, b_ref, q_ref, kd_ref, vd_ref, u_ref, *, attn_w, kv_w, conv_w):
    xb = x_ref[...].astype(BF16)
    lane = lax.broadcasted_iota(jnp.int32, (1, LANES), 1)
    first_half = (lane % HEAD_DIM) < (HEAD_DIM // 2)
    low_head = lane < HEAD_DIM
    ang = pos_ref[...].astype(F32) * invf_ref[...]
    cos = jnp.cos(ang)
    sin = jnp.sin(ang)
    sin_signed = jnp.where(first_half, -sin, sin)

    def rope(h):
        rot = jnp.where(first_half, pltpu.roll(h, LANES - HEAD_DIM // 2, 1), pltpu.roll(h, HEAD_DIM // 2, 1))
        return h * cos + rot * sin_signed

    def proj(c0, width):
        return jnp.dot(xb, w_ref[:, c0:c0 + width], preferred_element_type=F32) + b_ref[:, c0:c0 + width]

    chunk = 4 * LANES
    for c0 in range(0, attn_w, chunk):
        h = proj(c0, chunk)
        for s in range(0, chunk, LANES):
            q_ref[:, c0 + s:c0 + s + LANES] = (rope(h[:, s:s + LANES]) * (HEAD_DIM ** -0.5)).astype(BF16)

    hkv = proj(attn_w, 2 * kv_w)
    k = rope(hkv[:, :kv_w])
    v = hkv[:, kv_w:]
    for src, dst in ((k, kd_ref), (v, vd_ref)):
        swapped = pltpu.roll(src, HEAD_DIM, 1)
        dst[:, :LANES] = jnp.where(low_head, src, swapped).astype(BF16)
        dst[:, LANES:] = jnp.where(low_head, swapped, src).astype(BF16)

    o_a = attn_w + 2 * kv_w
    o_g = o_a + conv_w
    for c0 in range(0, conv_w, chunk):
        a = proj(o_a + c0, chunk)
        g = proj(o_g + c0, chunk)
        u_ref[:, c0:c0 + chunk] = a * jax.nn.sigmoid(g)


def _in_proj(x2, pos2, invf, w, b, *, tm):
    t, d = x2.shape
    attn_w = N_Q_HEADS * HEAD_DIM
    kv_w = N_KV_HEADS * HEAD_DIM
    conv_w = (w.shape[1] - attn_w - 2 * kv_w) // 2
    assert kv_w == LANES and attn_w % (4 * LANES) == 0 and conv_w % (4 * LANES) == 0
    row = lambda width: pl.BlockSpec((tm, width), lambda i: (i, 0))
    return pl.pallas_call(
        functools.partial(_in_proj_kernel, attn_w=attn_w, kv_w=kv_w, conv_w=conv_w),
        grid=(t // tm,),
        in_specs=[row(d), row(1), _resident(invf.shape), _resident(w.shape), _resident(b.shape)],
        out_specs=[row(attn_w), row(2 * kv_w), row(2 * kv_w), row(conv_w)],
        out_shape=[jax.ShapeDtypeStruct((t, attn_w), BF16), jax.ShapeDtypeStruct((t, 2 * kv_w), BF16),
                   jax.ShapeDtypeStruct((t, 2 * kv_w), BF16), jax.ShapeDtypeStruct((t, conv_w), F32)],
        compiler_params=pltpu.CompilerParams(dimension_semantics=("arbitrary",), vmem_limit_bytes=VMEM_LIMIT_BYTES),
        name="in_proj",
    )(x2, pos2, invf, w, b)


def _attn_kernel(sinks_ref, q_ref, kc_ref, kp_ref, vc_ref, vp_ref, o_ref, kfull_ref, vfull_ref, *, tq):
    i = pl.program_id(1)
    kfull_ref[:BLOCK, :] = kp_ref[...]
    kfull_ref[BLOCK:, :] = kc_ref[...]
    vfull_ref[:BLOCK, :] = vp_ref[...]
    vfull_ref[BLOCK:, :] = vc_ref[...]

    lane = lax.broadcasted_iota(jnp.int32, (1, LANES), 1)
    low_head = lane < HEAD_DIM
    qi = lax.broadcasted_iota(jnp.int32, (BLOCK, 2 * BLOCK), 0)
    si = lax.broadcasted_iota(jnp.int32, (BLOCK, 2 * BLOCK), 1)
    diff = qi + BLOCK - si
    band = (diff >= 0) & (diff < WINDOW)
    chunks_per_kv = (N_Q_HEADS // N_KV_HEADS) * HEAD_DIM // LANES
    n_chunks = N_KV_HEADS * chunks_per_kv
    zero = jnp.zeros((), BF16)
    sink_col = lax.broadcasted_iota(jnp.int32, (1, 2 * BLOCK), 1) == 0
    sink_row = lax.broadcasted_iota(jnp.int32, (2 * BLOCK, LANES), 0) == 0

    def block_body(j, carry):
        r0 = pl.multiple_of(j * BLOCK, BLOCK)
        first_key = jnp.where((i > 0) | (j > 0), 0, BLOCK)
        valid = band & (si >= first_key)
        k2, v2 = [], []
        for g in range(N_KV_HEADS):
            kk = kfull_ref[pl.ds(r0, 2 * BLOCK), g * LANES:(g + 1) * LANES]
            vv = vfull_ref[pl.ds(r0, 2 * BLOCK), g * LANES:(g + 1) * LANES]
            vv = jnp.where(sink_row, zero, vv)
            k2.append(jnp.concatenate([jnp.where(low_head, kk, zero), jnp.where(low_head, zero, kk)], axis=0))
            v2.append(jnp.concatenate([jnp.where(low_head, vv, zero), jnp.where(low_head, zero, vv)], axis=0))

        def scores(c):
            qc = q_ref[pl.ds(r0, BLOCK), c * LANES:(c + 1) * LANES]
            return lax.dot_general(qc, k2[c // chunks_per_kv], (((1,), (1,)), ((), ())), preferred_element_type=F32)

        s_all = [scores(c) for c in range(n_chunks)]
        for c in range(n_chunks):
            s2 = s_all[c]
            probs = []
            for hh in range(2):
                fill = jnp.where(sink_col, sinks_ref[2 * c + hh], -jnp.inf)
                s = jnp.where(valid, s2[:, hh * 2 * BLOCK:(hh + 1) * 2 * BLOCK], fill)
                m = jnp.max(s, axis=-1, keepdims=True)
                p = jnp.exp(s - m)
                denom = jnp.sum(p, axis=-1, keepdims=True)
                probs.append((p * (1.0 / denom)).astype(BF16))
            p2 = jnp.concatenate(probs, axis=1)
            o = jnp.dot(p2, v2[c // chunks_per_kv], preferred_element_type=F32)
            o_ref[pl.ds(r0, BLOCK), c * LANES:(c + 1) * LANES] = o.astype(BF16)
        return carry

    lax.fori_loop(0, tq // BLOCK, block_body, 0)


def _attention(sinks, q, kd, vd, *, batch, seq, tq):
    t, attn_w = q.shape
    kvw = kd.shape[1]
    tiles = seq // tq
    blocks_per_tile = tq // BLOCK
    cur = lambda width: pl.BlockSpec((tq, width), lambda b, i: (b * tiles + i, 0))
    prev = pl.BlockSpec((BLOCK, kvw),
                        lambda b, i: (b * (seq // BLOCK) + jnp.maximum(i * blocks_per_tile - 1, 0), 0))
    return pl.pallas_call(
        functools.partial(_attn_kernel, tq=tq),
        grid=(batch, tiles),
        in_specs=[pl.BlockSpec(memory_space=pltpu.SMEM), cur(attn_w), cur(kvw), prev, cur(kvw), prev],
        out_specs=cur(attn_w),
        out_shape=jax.ShapeDtypeStruct((t, attn_w), BF16),
        scratch_shapes=[pltpu.VMEM((tq + BLOCK, kvw), BF16), pltpu.VMEM((tq + BLOCK, kvw), BF16)],
        compiler_params=pltpu.CompilerParams(dimension_semantics=("arbitrary", "arbitrary"),
                                             vmem_limit_bytes=VMEM_LIMIT_BYTES),
        name="swa_attention",
    )(sinks, q, kd, kd, vd, vd)


def _conv_kernel(uc_ref, up_ref, wdw_ref, bdw_ref, lng_ref, lnb_ref, wpw_ref, bpw_ref, o_ref, ext_ref, y_ref, *, tc, rows):
    i = pl.program_id(1)
    halo = up_ref[...]
    ext_ref[:CONV_HALO, :] = jnp.where(i > 0, halo, jnp.zeros_like(halo))
    ext_ref[CONV_HALO:, :] = uc_ref[...]
    width = uc_ref.shape[1]
    shift0 = CONV_HALO - (CONV_KERNEL - 1)

    for c0 in range(0, width, LANES):
        w = wdw_ref[:, c0:c0 + LANES]
        bias = bdw_ref[:, c0:c0 + LANES]

        def row_body(r, carry, c0=c0, w=w, bias=bias):
            base = pl.multiple_of(r * rows, rows)
            win = ext_ref[pl.ds(base, rows + CONV_HALO), c0:c0 + LANES]
            acc = jnp.broadcast_to(bias, (rows, LANES))
            for b in range(SUBLANES):
                wb = win if b == 0 else pltpu.roll(win, rows + CONV_HALO - b, 0)
                for a in range(CONV_HALO // SUBLANES + 1):
                    j = SUBLANES * a + b - shift0
                    if 0 <= j < CONV_KERNEL:
                        acc = acc + wb[SUBLANES * a:SUBLANES * a + rows, :] * w[j:j + 1, :]
            y_ref[pl.ds(base, rows), c0:c0 + LANES] = acc
            return carry

        lax.fori_loop(0, tc // rows, row_body, 0)

    y = _layer_norm(y_ref[...], lng_ref[...], lnb_ref[...])
    act = jax.nn.silu(y).astype(BF16)
    o_ref[...] = (jnp.dot(act, wpw_ref[...], preferred_element_type=F32) + bpw_ref[...]).astype(BF16)


def _conv_branch(u, wdw, bdw, lng, lnb, wpw, bpw, *, batch, seq, tc, rows=64):
    t, width = u.shape
    tiles = seq // tc
    cur = pl.BlockSpec((tc, width), lambda b, i: (b * tiles + i, 0))
    prev = pl.BlockSpec((CONV_HALO, width),
                        lambda b, i: (b * (seq // CONV_HALO) + jnp.maximum(i * (tc // CONV_HALO) - 1, 0), 0))
    return pl.pallas_call(
        functools.partial(_conv_kernel, tc=tc, rows=rows),
        grid=(batch, tiles),
        in_specs=[cur, prev, _resident(wdw.shape), _resident(bdw.shape), _resident(lng.shape), _resident(lnb.shape),
                  _resident(wpw.shape), _resident(bpw.shape)],
        out_specs=cur,
        out_shape=jax.ShapeDtypeStruct((t, width), BF16),
        scratch_shapes=[pltpu.VMEM((tc + CONV_HALO, width), F32), pltpu.VMEM((tc, width), F32)],
        compiler_params=pltpu.CompilerParams(dimension_semantics=("arbitrary", "arbitrary"),
                                             vmem_limit_bytes=VMEM_LIMIT_BYTES),
        name="conformer_conv",
    )(u, u, wdw, bdw, lng, lnb, wpw, bpw)


def _out_proj_kernel(a_ref, c_ref, x_ref, w_ref, b_ref, g_ref, beta_ref, o_ref, *, alpha):
    attn_w = a_ref.shape[1]
    rows = a_ref.shape[0] // 2
    for r0 in (0, rows):
        mix = jnp.dot(a_ref[r0:r0 + rows, :], w_ref[:attn_w, :], preferred_element_type=F32)
        mix = mix + jnp.dot(c_ref[r0:r0 + rows, :], w_ref[attn_w:, :], preferred_element_type=F32)
        y = alpha * x_ref[r0:r0 + rows, :] + (mix + b_ref[...])
        o_ref[r0:r0 + rows, :] = _layer_norm(y, g_ref[...], beta_ref[...])


def _out_proj(attn, conv, x2, w, b, g, beta, *, alpha, tm):
    t, d = x2.shape
    row = lambda width: pl.BlockSpec((tm, width), lambda i: (i, 0))
    return pl.pallas_call(
        functools.partial(_out_proj_kernel, alpha=alpha),
        grid=(t // tm,),
        in_specs=[row(attn.shape[1]), row(conv.shape[1]), row(d), _resident(w.shape), _resident(b.shape),
                  _resident(g.shape), _resident(beta.shape)],
        out_specs=row(d),
        out_shape=jax.ShapeDtypeStruct((t, d), F32),
        compiler_params=pltpu.CompilerParams(dimension_semantics=("arbitrary",), vmem_limit_bytes=VMEM_LIMIT_BYTES),
        name="out_proj_ln",
    )(attn, conv, x2, w, b, g, beta)


def _ffn_kernel(x_ref, wg_ref, wu_ref, wd_ref, g_ref, beta_ref, o_ref, xb_ref, *, alpha):
    f = pl.program_id(1)

    @pl.when(f == 0)
    def _():
        xb_ref[...] = x_ref[...].astype(BF16)

    @pl.when(f == 0)
    def _():
        o_ref[...] = alpha * x_ref[...]

    xb = xb_ref[...]
    tf = wg_ref.shape[2]
    hidden = []
    for c0 in range(0, tf, tf // 2):
        gate = jnp.dot(xb, wg_ref[0, :, c0:c0 + tf // 2], preferred_element_type=F32)
        up = jnp.dot(xb, wu_ref[0, :, c0:c0 + tf // 2], preferred_element_type=F32)
        hidden.append((jax.nn.silu(gate) * up).astype(BF16))
    half = o_ref.shape[1] // 2
    for h, c0 in zip(hidden, range(0, tf, tf // 2)):
        for n0 in (0, half):
            o_ref[:, n0:n0 + half] += jnp.dot(h, wd_ref[c0:c0 + tf // 2, n0:n0 + half], preferred_element_type=F32)

    @pl.when(f == pl.num_programs(1) - 1)
    def _():
        o_ref[...] = _layer_norm(o_ref[...], g_ref[...], beta_ref[...])


def _ffn(x1, wg, wu, wd, g, beta, *, alpha, tm):
    t, d = x1.shape
    nf, _, tf = wg.shape
    return pl.pallas_call(
        functools.partial(_ffn_kernel, alpha=alpha),
        grid=(t // tm, nf),
        in_specs=[pl.BlockSpec((tm, d), lambda i, f: (i, 0)),
                  pl.BlockSpec((1, d, tf), lambda i, f: (f, 0, 0)),
                  pl.BlockSpec((1, d, tf), lambda i, f: (f, 0, 0)),
                  pl.BlockSpec((tf, d), lambda i, f: (f, 0)),
                  _resident(g.shape), _resident(beta.shape)],
        out_specs=pl.BlockSpec((tm, d), lambda i, f: (i, 0)),
        out_shape=jax.ShapeDtypeStruct((t, d), F32),
        scratch_shapes=[pltpu.VMEM((tm, d), BF16)],
        compiler_params=pltpu.CompilerParams(dimension_semantics=("arbitrary", "arbitrary"),
                                             vmem_limit_bytes=VMEM_LIMIT_BYTES),
        name="swiglu_ffn_ln",
    )(x1, wg, wu, wd, g, beta)


def _chunk_columns(w, tf):
    d, f = w.shape
    return w.reshape(d, f // tf, tf).transpose(1, 0, 2)


def kernel(x, positions, w_in, b_in, sinks, w_dw, b_dw, conv_ln_g, conv_ln_b, w_pw2, b_pw2, w_out, b_out, ln1_g,
           ln1_b, w_gate, w_up, w_down, ln2_g, ln2_b):
    batch, seq, d = x.shape
    depth = w_in.shape[0]
    alpha = (2 * depth) ** 0.25
    t = batch * seq
    tf = FFN_CHUNK
    row = lambda a: a.reshape(1, -1).astype(F32)

    half = HEAD_DIM // 2
    inv_freq = 1.0 / (ROPE_THETA ** (jnp.arange(half, dtype=F32) * 2.0 / HEAD_DIM))
    invf = jnp.tile(inv_freq, LANES // half).reshape(1, LANES)
    pos2 = positions.reshape(t, 1)

    x2 = x.reshape(t, d)
    for l in range(depth):
        q, kd, vd, u = _in_proj(x2, pos2, invf, w_in[l].astype(BF16), row(b_in[l]), tm=TOKEN_TILE)
        attn = _attention(sinks[l].astype(F32), q, kd, vd, batch=batch, seq=seq, tq=ATTN_TILE)
        wdw = w_dw[l].reshape(CONV_KERNEL, -1).astype(F32)
        conv = _conv_branch(u, wdw, row(b_dw[l]), row(conv_ln_g[l]), row(conv_ln_b[l]), w_pw2[l].astype(BF16),
                            row(b_pw2[l]), batch=batch, seq=seq, tc=CONV_TILE)
        x2 = _out_proj(attn, conv, x2, w_out[l].astype(BF16), row(b_out[l]), row(ln1_g[l]), row(ln1_b[l]),
                       alpha=alpha, tm=TOKEN_TILE)
        x2 = _ffn(x2, _chunk_columns(w_gate[l].astype(BF16), tf), _chunk_columns(w_up[l].astype(BF16), tf),
                  w_down[l].astype(BF16), row(ln2_g[l]), row(ln2_b[l]), alpha=alpha, tm=TOKEN_TILE)
    return x2.reshape(batch, seq, d)
```

```python
import functools

import jax
import jax.numpy as jnp
from jax import lax
from jax.experimental import pallas as pl
from jax.experimental.pallas import tpu as pltpu

HEAD_DIM = 64
N_Q_HEADS = 16
N_KV_HEADS = 2
WINDOW = 128
BLOCK = 128
CONV_KERNEL = 31
ROPE_THETA = 10000.0
LN_EPS = 1e-5

LANES = 128
SUBLANES = 8
CONV_HALO = 32
CONV_STRIDE = 4
VMEM_LIMIT_BYTES = 60 * 1024 * 1024

TOKEN_TILE = 1024
ATTN_TILE = 512
CONV_TILE = 512
FFN_CHUNK = 512

F32 = jnp.float32
BF16 = jnp.bfloat16


def _resident(shape):
    return pl.BlockSpec(shape, lambda *_: (0,) * len(shape), pipeline_mode=pl.Buffered(1))


def _layer_norm(y, g, b):
    mu = jnp.mean(y, axis=-1, keepdims=True)
    d = y - mu
    var = jnp.mean(d * d, axis=-1, keepdims=True)
    return d * lax.rsqrt(var + LN_EPS) * g + b


def _in_proj_kernel(x_ref, pos_ref, invf_ref, w_ref, b_ref, q_ref, kd_ref, vd_ref, u_ref, *, attn_w, kv_w, conv_w):
    xb = x_ref[...].astype(BF16)
    lane = lax.broadcasted_iota(jnp.int32, (1, LANES), 1)
    first_half = (lane % HEAD_DIM) < (HEAD_DIM // 2)
    low_head = lane < HEAD_DIM
    ang = pos_ref[...].astype(F32) * invf_ref[...]
    cos = jnp.cos(ang)
    sin = jnp.sin(ang)
    sin_signed = jnp.where(first_half, -sin, sin)

    def rope(h):
        rot = jnp.where(first_half, pltpu.roll(h, LANES - HEAD_DIM // 2, 1), pltpu.roll(h, HEAD_DIM // 2, 1))
        return h * cos + rot * sin_signed

    def proj(c0, width):
        return jnp.dot(xb, w_ref[:, c0:c0 + width], preferred_element_type=F32) + b_ref[:, c0:c0 + width]

    chunk = 4 * LANES
    for c0 in range(0, attn_w, chunk):
        h = proj(c0, chunk)
        for s in range(0, chunk, LANES):
            q_ref[:, c0 + s:c0 + s + LANES] = (rope(h[:, s:s + LANES]) * (HEAD_DIM ** -0.5)).astype(BF16)

    hkv = proj(attn_w, 2 * kv_w)
    k = rope(hkv[:, :kv_w])
    v = hkv[:, kv_w:]
    for src, dst in ((k, kd_ref), (v, vd_ref)):
        swapped = pltpu.roll(src, HEAD_DIM, 1)
        dst[:, :LANES] = jnp.where(low_head, src, swapped).astype(BF16)
        dst[:, LANES:] = jnp.where(low_head, swapped, src).astype(BF16)

    o_a = attn_w + 2 * kv_w
    o_g = o_a + conv_w
    for c0 in range(0, conv_w, chunk):
        a = proj(o_a + c0, chunk)
        g = proj(o_g + c0, chunk)
        u_ref[:, c0:c0 + chunk] = a * jax.nn.sigmoid(g)


def _in_proj(x2, pos2, invf, w, b, *, tm):
    t, d = x2.shape
    attn_w = N_Q_HEADS * HEAD_DIM
    kv_w = N_KV_HEADS * HEAD_DIM
    conv_w = (w.shape[1] - attn_w - 2 * kv_w) // 2
    assert kv_w == LANES and attn_w % (4 * LANES) == 0 and conv_w % (4 * LANES) == 0
    row = lambda width: pl.BlockSpec((tm, width), lambda i: (i, 0))
    return pl.pallas_call(
        functools.partial(_in_proj_kernel, attn_w=attn_w, kv_w=kv_w, conv_w=conv_w),
        grid=(t // tm,),
        in_specs=[row(d), row(1), _resident(invf.shape), _resident(w.shape), _resident(b.shape)],
        out_specs=[row(attn_w), row(2 * kv_w), row(2 * kv_w), row(conv_w)],
        out_shape=[jax.ShapeDtypeStruct((t, attn_w), BF16), jax.ShapeDtypeStruct((t, 2 * kv_w), BF16),
                   jax.ShapeDtypeStruct((t, 2 * kv_w), BF16), jax.ShapeDtypeStruct((t, conv_w), F32)],
        compiler_params=pltpu.CompilerParams(dimension_semantics=("arbitrary",), vmem_limit_bytes=VMEM_LIMIT_BYTES),
        name="in_proj",
    )(x2, pos2, invf, w, b)


def _attn_kernel(sinks_ref, q_ref, kc_ref, kp_ref, vc_ref, vp_ref, o_ref, kfull_ref, vfull_ref, *, tq):
    i = pl.program_id(1)
    kfull_ref[:BLOCK, :] = kp_ref[...]
    kfull_ref[BLOCK:, :] = kc_ref[...]
    vfull_ref[:BLOCK, :] = vp_ref[...]
    vfull_ref[BLOCK:, :] = vc_ref[...]

    lane = lax.broadcasted_iota(jnp.int32, (1, LANES), 1)
    low_head = lane < HEAD_DIM
    qi = lax.broadcasted_iota(jnp.int32, (BLOCK, 2 * BLOCK), 0)
    si = lax.broadcasted_iota(jnp.int32, (BLOCK, 2 * BLOCK), 1)
    diff = qi + BLOCK - si
    band = (diff >= 0) & (diff < WINDOW)
    chunks_per_kv = (N_Q_HEADS // N_KV_HEADS) * HEAD_DIM // LANES
    n_chunks = N_KV_HEADS * chunks_per_kv
    zero = jnp.zeros((), BF16)
    sink_col = lax.broadcasted_iota(jnp.int32, (1, 2 * BLOCK), 1) == 0
    sink_row = lax.broadcasted_iota(jnp.int32, (2 * BLOCK, LANES), 0) == 0

    def block_body(j, carry):
        r0 = pl.multiple_of(j * BLOCK, BLOCK)
        first_key = jnp.where((i > 0) | (j > 0), 0, BLOCK)
        valid = band & (si >= first_key)
        k2, v2 = [], []
        for g in range(N_KV_HEADS):
            kk = kfull_ref[pl.ds(r0, 2 * BLOCK), g * LANES:(g + 1) * LANES]
            vv = vfull_ref[pl.ds(r0, 2 * BLOCK), g * LANES:(g + 1) * LANES]
            vv = jnp.where(sink_row, zero, vv)
            k2.append(jnp.concatenate([jnp.where(low_head, kk, zero), jnp.where(low_head, zero, kk)], axis=0))
            v2.append(jnp.concatenate([jnp.where(low_head, vv, zero), jnp.where(low_head, zero, vv)], axis=0))

        def scores(c):
            qc = q_ref[pl.ds(r0, BLOCK), c * LANES:(c + 1) * LANES]
            return lax.dot_general(qc, k2[c // chunks_per_kv], (((1,), (1,)), ((), ())), preferred_element_type=F32)

        s_all = [scores(c) for c in range(n_chunks)]
        for c in range(n_chunks):
            s2 = s_all[c]
            probs = []
            for hh in range(2):
                fill = jnp.where(sink_col, sinks_ref[2 * c + hh], -jnp.inf)
                s = jnp.where(valid, s2[:, hh * 2 * BLOCK:(hh + 1) * 2 * BLOCK], fill)
                m = jnp.max(s, axis=-1, keepdims=True)
                p = jnp.exp(s - m)
                denom = jnp.sum(p, axis=-1, keepdims=True)
                probs.append((p * (1.0 / denom)).astype(BF16))
            p2 = jnp.concatenate(probs, axis=1)
            o = jnp.dot(p2, v2[c // chunks_per_kv], preferred_element_type=F32)
            o_ref[pl.ds(r0, BLOCK), c * LANES:(c + 1) * LANES] = o.astype(BF16)
        return carry

    lax.fori_loop(0, tq // BLOCK, block_body, 0)


def _attention(sinks, q, kd, vd, *, batch, seq, tq):
    t, attn_w = q.shape
    kvw = kd.shape[1]
    tiles = seq // tq
    blocks_per_tile = tq // BLOCK
    cur = lambda width: pl.BlockSpec((tq, width), lambda b, i: (b * tiles + i, 0))
    prev = pl.BlockSpec((BLOCK, kvw),
                        lambda b, i: (b * (seq // BLOCK) + jnp.maximum(i * blocks_per_tile - 1, 0), 0))
    return pl.pallas_call(
        functools.partial(_attn_kernel, tq=tq),
        grid=(batch, tiles),
        in_specs=[pl.BlockSpec(memory_space=pltpu.SMEM), cur(attn_w), cur(kvw), prev, cur(kvw), prev],
        out_specs=cur(attn_w),
        out_shape=jax.ShapeDtypeStruct((t, attn_w), BF16),
        scratch_shapes=[pltpu.VMEM((tq + BLOCK, kvw), BF16), pltpu.VMEM((tq + BLOCK, kvw), BF16)],
        compiler_params=pltpu.CompilerParams(dimension_semantics=("arbitrary", "arbitrary"),
                                             vmem_limit_bytes=VMEM_LIMIT_BYTES),
        name="swa_attention",
    )(sinks, q, kd, kd, vd, vd)


def _conv_kernel(uc_ref, up_ref, wdw_ref, bdw_ref, lng_ref, lnb_ref, wpw_ref, bpw_ref, o_ref, ext_ref, y_ref, *, tc):
    i = pl.program_id(1)
    n_slabs = uc_ref.shape[1] // LANES
    halo = up_ref[...]
    halo = jnp.where(i > 0, halo, jnp.zeros_like(halo))
    for c in range(n_slabs):
        ext_ref[c, :CONV_HALO, :] = halo[:, c * LANES:(c + 1) * LANES]
        ext_ref[c, CONV_HALO:, :] = uc_ref[:, c * LANES:(c + 1) * LANES]

    shift0 = CONV_HALO - (CONV_KERNEL - 1)
    unit = SUBLANES * CONV_STRIDE
    passes = ((0, (CONV_KERNEL + 1) // 2), ((CONV_KERNEL + 1) // 2, CONV_KERNEL))
    for c in range(n_slabs):
        for first, (j0, j1) in zip((True, False), passes):
            w_rows = {j: wdw_ref[SUBLANES * j:SUBLANES * (j + 1), c * LANES:(c + 1) * LANES] for j in range(j0, j1)}
            bias = jnp.broadcast_to(bdw_ref[:, c * LANES:(c + 1) * LANES], (SUBLANES, LANES))

            def unit_body(r, carry, c=c, first=first, j0=j0, j1=j1, w_rows=w_rows, bias=bias):
                base = r * unit
                out = [y_ref.at[c, pl.ds(base + k, SUBLANES, stride=CONV_STRIDE), :] for k in range(CONV_STRIDE)]
                acc = [[bias if first else out[k][...], None] for k in range(CONV_STRIDE)]
                for jk in range(j0, j1 + CONV_STRIDE - 1):
                    v = ext_ref[c, pl.ds(base + shift0 + jk, SUBLANES, stride=CONV_STRIDE), :]
                    for k in range(CONV_STRIDE):
                        j = jk - k
                        if j0 <= j < j1:
                            term = v * w_rows[j]
                            acc[k][j % 2] = term if acc[k][j % 2] is None else acc[k][j % 2] + term
                for k in range(CONV_STRIDE):
                    out[k][...] = acc[k][0] + acc[k][1]
                return carry

            lax.fori_loop(0, tc // unit, unit_body, 0, unroll=4)

    y = jnp.concatenate([y_ref[c] for c in range(n_slabs)], axis=1)
    y = _layer_norm(y, lng_ref[...], lnb_ref[...])
    act = jax.nn.silu(y).astype(BF16)
    o_ref[...] = (jnp.dot(act, wpw_ref[...], preferred_element_type=F32) + bpw_ref[...]).astype(BF16)


def _conv_branch(u, wdw, bdw, lng, lnb, wpw, bpw, *, batch, seq, tc):
    t, width = u.shape
    tiles = seq // tc
    cur = pl.BlockSpec((tc, width), lambda b, i: (b * tiles + i, 0))
    prev = pl.BlockSpec((CONV_HALO, width),
                        lambda b, i: (b * (seq // CONV_HALO) + jnp.maximum(i * (tc // CONV_HALO) - 1, 0), 0))
    return pl.pallas_call(
        functools.partial(_conv_kernel, tc=tc),
        grid=(batch, tiles),
        in_specs=[cur, prev, _resident(wdw.shape), _resident(bdw.shape), _resident(lng.shape), _resident(lnb.shape),
                  _resident(wpw.shape), _resident(bpw.shape)],
        out_specs=cur,
        out_shape=jax.ShapeDtypeStruct((t, width), BF16),
        scratch_shapes=[pltpu.VMEM((width // LANES, tc + CONV_HALO, LANES), F32),
                        pltpu.VMEM((width // LANES, tc, LANES), F32)],
        compiler_params=pltpu.CompilerParams(dimension_semantics=("arbitrary", "arbitrary"),
                                             vmem_limit_bytes=VMEM_LIMIT_BYTES),
        name="conformer_conv",
    )(u, u, wdw, bdw, lng, lnb, wpw, bpw)


def _out_proj_kernel(a_ref, c_ref, x_ref, w_ref, b_ref, g_ref, beta_ref, o_ref, *, alpha):
    attn_w = a_ref.shape[1]
    rows = a_ref.shape[0] // 2
    for r0 in (0, rows):
        mix = jnp.dot(a_ref[r0:r0 + rows, :], w_ref[:attn_w, :], preferred_element_type=F32)
        mix = mix + jnp.dot(c_ref[r0:r0 + rows, :], w_ref[attn_w:, :], preferred_element_type=F32)
        y = alpha * x_ref[r0:r0 + rows, :] + (mix + b_ref[...])
        o_ref[r0:r0 + rows, :] = _layer_norm(y, g_ref[...], beta_ref[...])


def _out_proj(attn, conv, x2, w, b, g, beta, *, alpha, tm):
    t, d = x2.shape
    row = lambda width: pl.BlockSpec((tm, width), lambda i: (i, 0))
    return pl.pallas_call(
        functools.partial(_out_proj_kernel, alpha=alpha),
        grid=(t // tm,),
        in_specs=[row(attn.shape[1]), row(conv.shape[1]), row(d), _resident(w.shape), _resident(b.shape),
                  _resident(g.shape), _resident(beta.shape)],
        out_specs=row(d),
        out_shape=jax.ShapeDtypeStruct((t, d), F32),
        compiler_params=pltpu.CompilerParams(dimension_semantics=("arbitrary",), vmem_limit_bytes=VMEM_LIMIT_BYTES),
        name="out_proj_ln",
    )(attn, conv, x2, w, b, g, beta)


def _ffn_kernel(x_ref, wg_ref, wu_ref, wd_ref, g_ref, beta_ref, o_ref, xb_ref, *, alpha):
    f = pl.program_id(1)

    @pl.when(f == 0)
    def _():
        xb_ref[...] = x_ref[...].astype(BF16)

    @pl.when(f == 0)
    def _():
        o_ref[...] = alpha * x_ref[...]

    xb = xb_ref[...]
    tf = wg_ref.shape[1]
    hidden = []
    for c0 in range(0, tf, tf // 2):
        gate = jnp.dot(xb, wg_ref[:, c0:c0 + tf // 2], preferred_element_type=F32)
        up = jnp.dot(xb, wu_ref[:, c0:c0 + tf // 2], preferred_element_type=F32)
        hidden.append((jax.nn.silu(gate) * up).astype(BF16))
    half = o_ref.shape[1] // 2
    for h, c0 in zip(hidden, range(0, tf, tf // 2)):
        for n0 in (0, half):
            o_ref[:, n0:n0 + half] += jnp.dot(h, wd_ref[c0:c0 + tf // 2, n0:n0 + half], preferred_element_type=F32)

    @pl.when(f == pl.num_programs(1) - 1)
    def _():
        o_ref[...] = _layer_norm(o_ref[...], g_ref[...], beta_ref[...])


def _ffn(x1, wg, wu, wd, g, beta, *, alpha, tm, tf):
    t, d = x1.shape
    nf = wg.shape[1] // tf
    return pl.pallas_call(
        functools.partial(_ffn_kernel, alpha=alpha),
        grid=(t // tm, nf),
        in_specs=[pl.BlockSpec((tm, d), lambda i, f: (i, 0)),
                  pl.BlockSpec((d, tf), lambda i, f: (0, f)),
                  pl.BlockSpec((d, tf), lambda i, f: (0, f)),
                  pl.BlockSpec((tf, d), lambda i, f: (f, 0)),
                  _resident(g.shape), _resident(beta.shape)],
        out_specs=pl.BlockSpec((tm, d), lambda i, f: (i, 0)),
        out_shape=jax.ShapeDtypeStruct((t, d), F32),
        scratch_shapes=[pltpu.VMEM((tm, d), BF16)],
        compiler_params=pltpu.CompilerParams(dimension_semantics=("arbitrary", "arbitrary"),
                                             vmem_limit_bytes=VMEM_LIMIT_BYTES),
        name="swiglu_ffn_ln",
    )(x1, wg, wu, wd, g, beta)


def kernel(x, positions, w_in, b_in, sinks, w_dw, b_dw, conv_ln_g, conv_ln_b, w_pw2, b_pw2, w_out, b_out, ln1_g,
           ln1_b, w_gate, w_up, w_down, ln2_g, ln2_b):
    batch, seq, d = x.shape
    depth = w_in.shape[0]
    alpha = (2 * depth) ** 0.25
    t = batch * seq
    row = lambda a: a.reshape(1, -1).astype(F32)

    half = HEAD_DIM // 2
    inv_freq = 1.0 / (ROPE_THETA ** (jnp.arange(half, dtype=F32) * 2.0 / HEAD_DIM))
    invf = jnp.tile(inv_freq, LANES // half).reshape(1, LANES)
    pos2 = positions.reshape(t, 1)

    x2 = x.reshape(t, d)
    for l in range(depth):
        q, kd, vd, u = _in_proj(x2, pos2, invf, w_in[l].astype(BF16), row(b_in[l]), tm=TOKEN_TILE)
        attn = _attention(sinks[l].astype(F32), q, kd, vd, batch=batch, seq=seq, tq=ATTN_TILE)
        wdw = jnp.repeat(w_dw[l].reshape(CONV_KERNEL, -1).astype(F32), SUBLANES, axis=0)
        conv = _conv_branch(u, wdw, row(b_dw[l]), row(conv_ln_g[l]), row(conv_ln_b[l]), w_pw2[l].astype(BF16),
                            row(b_pw2[l]), batch=batch, seq=seq, tc=CONV_TILE)
        x2 = _out_proj(attn, conv, x2, w_out[l].astype(BF16), row(b_out[l]), row(ln1_g[l]), row(ln1_b[l]),
                       alpha=alpha, tm=TOKEN_TILE)
        x2 = _ffn(x2, w_gate[l].astype(BF16), w_up[l].astype(BF16), w_down[l].astype(BF16), row(ln2_g[l]),
                  row(ln2_b[l]), alpha=alpha, tm=TOKEN_TILE, tf=FFN_CHUNK)
    return x2.reshape(batch, seq, d)
```

```python
import functools

import jax
import jax.numpy as jnp
from jax import lax
from jax.experimental import pallas as pl
from jax.experimental.pallas import tpu as pltpu

HEAD_DIM = 64
N_Q_HEADS = 16
N_KV_HEADS = 2
WINDOW = 128
BLOCK = 128
CONV_KERNEL = 31
ROPE_THETA = 10000.0
LN_EPS = 1e-5

LANES = 128
SUBLANES = 8
CONV_HALO = 32
CONV_STRIDE = 4
VMEM_LIMIT_BYTES = 60 * 1024 * 1024

TOKEN_TILE = 1024
ATTN_TILE = 1024
CONV_TILE = 1024
CONV_TAIL_ROWS = 256
FFN_CHUNK = 512

F32 = jnp.float32
BF16 = jnp.bfloat16


def _resident(shape):
    return pl.BlockSpec(shape, lambda *_: (0,) * len(shape), pipeline_mode=pl.Buffered(1))


def _layer_norm(y, g, b):
    mu = jnp.mean(y, axis=-1, keepdims=True)
    d = y - mu
    var = jnp.mean(d * d, axis=-1, keepdims=True)
    return d * lax.rsqrt(var + LN_EPS) * g + b


def _in_proj_kernel(x_ref, pos_ref, invf_ref, w_ref, b_ref, q_ref, kd_ref, vd_ref, u_ref, *, attn_w, kv_w, conv_w):
    xb = x_ref[...].astype(BF16)
    lane = lax.broadcasted_iota(jnp.int32, (1, LANES), 1)
    first_half = (lane % HEAD_DIM) < (HEAD_DIM // 2)
    low_head = lane < HEAD_DIM
    ang = pos_ref[...].astype(F32) * invf_ref[...]
    cos = jnp.cos(ang)
    sin = jnp.sin(ang)
    sin_signed = jnp.where(first_half, -sin, sin)

    def rope(h):
        rot = jnp.where(first_half, pltpu.roll(h, LANES - HEAD_DIM // 2, 1), pltpu.roll(h, HEAD_DIM // 2, 1))
        return h * cos + rot * sin_signed

    def proj(c0, width):
        return jnp.dot(xb, w_ref[:, c0:c0 + width], preferred_element_type=F32) + b_ref[:, c0:c0 + width]

    chunk = 4 * LANES
    for c0 in range(0, attn_w, chunk):
        h = proj(c0, chunk)
        for s in range(0, chunk, LANES):
            q_ref[:, c0 + s:c0 + s + LANES] = (rope(h[:, s:s + LANES]) * (HEAD_DIM ** -0.5)).astype(BF16)

    hkv = proj(attn_w, 2 * kv_w)
    k = rope(hkv[:, :kv_w])
    v = hkv[:, kv_w:]
    for src, dst in ((k, kd_ref), (v, vd_ref)):
        swapped = pltpu.roll(src, HEAD_DIM, 1)
        dst[:, :LANES] = jnp.where(low_head, src, swapped).astype(BF16)
        dst[:, LANES:] = jnp.where(low_head, swapped, src).astype(BF16)

    o_a = attn_w + 2 * kv_w
    o_g = o_a + conv_w
    for c0 in range(0, conv_w, chunk):
        a = proj(o_a + c0, chunk)
        g = proj(o_g + c0, chunk)
        u_ref[:, c0:c0 + chunk] = a * jax.nn.sigmoid(g)


def _in_proj(x2, pos2, invf, w, b, *, tm):
    t, d = x2.shape
    attn_w = N_Q_HEADS * HEAD_DIM
    kv_w = N_KV_HEADS * HEAD_DIM
    conv_w = (w.shape[1] - attn_w - 2 * kv_w) // 2
    assert kv_w == LANES and attn_w % (4 * LANES) == 0 and conv_w % (4 * LANES) == 0
    row = lambda width: pl.BlockSpec((tm, width), lambda i: (i, 0))
    return pl.pallas_call(
        functools.partial(_in_proj_kernel, attn_w=attn_w, kv_w=kv_w, conv_w=conv_w),
        grid=(t // tm,),
        in_specs=[row(d), row(1), _resident(invf.shape), _resident(w.shape), _resident(b.shape)],
        out_specs=[row(attn_w), row(2 * kv_w), row(2 * kv_w), row(conv_w)],
        out_shape=[jax.ShapeDtypeStruct((t, attn_w), BF16), jax.ShapeDtypeStruct((t, 2 * kv_w), BF16),
                   jax.ShapeDtypeStruct((t, 2 * kv_w), BF16), jax.ShapeDtypeStruct((t, conv_w), F32)],
        compiler_params=pltpu.CompilerParams(dimension_semantics=("arbitrary",), vmem_limit_bytes=VMEM_LIMIT_BYTES),
        name="in_proj",
    )(x2, pos2, invf, w, b)


def _attn_kernel(sinks_ref, q_ref, kc_ref, kp_ref, vc_ref, vp_ref, o_ref, kfull_ref, vfull_ref, *, tq):
    i = pl.program_id(1)
    kfull_ref[:BLOCK, :] = kp_ref[...]
    kfull_ref[BLOCK:, :] = kc_ref[...]
    vfull_ref[:BLOCK, :] = vp_ref[...]
    vfull_ref[BLOCK:, :] = vc_ref[...]

    lane = lax.broadcasted_iota(jnp.int32, (1, LANES), 1)
    low_head = lane < HEAD_DIM
    qi = lax.broadcasted_iota(jnp.int32, (BLOCK, 2 * BLOCK), 0)
    si = lax.broadcasted_iota(jnp.int32, (BLOCK, 2 * BLOCK), 1)
    diff = qi + BLOCK - si
    band = (diff >= 0) & (diff < WINDOW)
    chunks_per_kv = (N_Q_HEADS // N_KV_HEADS) * HEAD_DIM // LANES
    n_chunks = N_KV_HEADS * chunks_per_kv
    zero = jnp.zeros((), BF16)
    sink_col = lax.broadcasted_iota(jnp.int32, (1, 2 * BLOCK), 1) == 0
    sink_row = lax.broadcasted_iota(jnp.int32, (2 * BLOCK, LANES), 0) == 0

    def split_heads(ref, r0, zero_sink_row):
        out = []
        for g in range(N_KV_HEADS):
            kv = ref[pl.ds(r0, 2 * BLOCK), g * LANES:(g + 1) * LANES]
            if zero_sink_row:
                kv = jnp.where(sink_row, zero, kv)
            out.append(jnp.concatenate([jnp.where(low_head, kv, zero), jnp.where(low_head, zero, kv)], axis=0))
        return out

    def block_body(j, carry):
        r0 = pl.multiple_of(j * BLOCK, BLOCK)
        first_key = jnp.where((i > 0) | (j > 0), 0, BLOCK)
        valid = band & (si >= first_key)
        k2 = split_heads(kfull_ref, r0, False)
        v2 = split_heads(vfull_ref, r0, True)
        s_all = [lax.dot_general(q_ref[pl.ds(r0, BLOCK), c * LANES:(c + 1) * LANES], k2[c // chunks_per_kv],
                                 (((1,), (1,)), ((), ())), preferred_element_type=F32) for c in range(n_chunks)]

        def logits(h):
            fill = jnp.where(sink_col, sinks_ref[h], -jnp.inf)
            return jnp.where(valid, s_all[h // 2][:, (h % 2) * 2 * BLOCK:(h % 2 + 1) * 2 * BLOCK], fill)

        heads = range(2 * n_chunks)
        row_max = [jnp.max(logits(h), axis=-1, keepdims=True) for h in heads]
        expo = [jnp.exp(logits(h) - row_max[h]) for h in heads]
        inv = [1.0 / jnp.sum(expo[h], axis=-1, keepdims=True) for h in heads]
        for c in range(n_chunks):
            p2 = jnp.concatenate([(expo[h] * inv[h]).astype(BF16) for h in (2 * c, 2 * c + 1)], axis=1)
            o = jnp.dot(p2, v2[c // chunks_per_kv], preferred_element_type=F32)
            o_ref[pl.ds(r0, BLOCK), c * LANES:(c + 1) * LANES] = o.astype(BF16)
        return carry

    lax.fori_loop(0, tq // BLOCK, block_body, 0)


def _attention(sinks, q, kd, vd, *, batch, seq, tq):
    t, attn_w = q.shape
    kvw = kd.shape[1]
    tiles = seq // tq
    blocks_per_tile = tq // BLOCK
    cur = lambda width: pl.BlockSpec((tq, width), lambda b, i: (b * tiles + i, 0))
    prev = pl.BlockSpec((BLOCK, kvw),
                        lambda b, i: (b * (seq // BLOCK) + jnp.maximum(i * blocks_per_tile - 1, 0), 0))
    return pl.pallas_call(
        functools.partial(_attn_kernel, tq=tq),
        grid=(batch, tiles),
        in_specs=[pl.BlockSpec(memory_space=pltpu.SMEM), cur(attn_w), cur(kvw), prev, cur(kvw), prev],
        out_specs=cur(attn_w),
        out_shape=jax.ShapeDtypeStruct((t, attn_w), BF16),
        scratch_shapes=[pltpu.VMEM((tq + BLOCK, kvw), BF16), pltpu.VMEM((tq + BLOCK, kvw), BF16)],
        compiler_params=pltpu.CompilerParams(dimension_semantics=("arbitrary", "arbitrary"),
                                             vmem_limit_bytes=VMEM_LIMIT_BYTES),
        name="swa_attention",
    )(sinks, q, kd, kd, vd, vd)


def _conv_kernel(uc_ref, up_ref, wdw_ref, bdw_ref, lng_ref, lnb_ref, wpw_ref, bpw_ref, o_ref, ext_ref, y_ref, *, tc):
    i = pl.program_id(1)
    n_slabs = uc_ref.shape[1] // LANES
    halo = up_ref[...]
    halo = jnp.where(i > 0, halo, jnp.zeros_like(halo))
    for c in range(n_slabs):
        ext_ref[c, :CONV_HALO, :] = halo[:, c * LANES:(c + 1) * LANES]
        ext_ref[c, CONV_HALO:, :] = uc_ref[:, c * LANES:(c + 1) * LANES]

    shift0 = CONV_HALO - (CONV_KERNEL - 1)
    unit = SUBLANES * CONV_STRIDE
    passes = ((0, (CONV_KERNEL + 1) // 2), ((CONV_KERNEL + 1) // 2, CONV_KERNEL))
    for c in range(n_slabs):
        for first, (j0, j1) in zip((True, False), passes):
            w_rows = {j: wdw_ref[SUBLANES * j:SUBLANES * (j + 1), c * LANES:(c + 1) * LANES] for j in range(j0, j1)}
            bias = jnp.broadcast_to(bdw_ref[:, c * LANES:(c + 1) * LANES], (SUBLANES, LANES))

            def unit_body(r, carry, c=c, first=first, j0=j0, j1=j1, w_rows=w_rows, bias=bias):
                base = r * unit
                out = [y_ref.at[c, pl.ds(base + k, SUBLANES, stride=CONV_STRIDE), :] for k in range(CONV_STRIDE)]
                acc = [[bias if first else out[k][...], None] for k in range(CONV_STRIDE)]
                for jk in range(j0, j1 + CONV_STRIDE - 1):
                    v = ext_ref[c, pl.ds(base + shift0 + jk, SUBLANES, stride=CONV_STRIDE), :]
                    for k in range(CONV_STRIDE):
                        j = jk - k
                        if j0 <= j < j1:
                            term = v * w_rows[j]
                            acc[k][j % 2] = term if acc[k][j % 2] is None else acc[k][j % 2] + term
                for k in range(CONV_STRIDE):
                    out[k][...] = acc[k][0] + acc[k][1]
                return carry

            lax.fori_loop(0, tc // unit, unit_body, 0, unroll=4)

    def activations(r0):
        y = jnp.concatenate([y_ref[c, r0:r0 + CONV_TAIL_ROWS, :] for c in range(n_slabs)], axis=1)
        return jax.nn.silu(_layer_norm(y, lng_ref[...], lnb_ref[...])).astype(BF16)

    act = activations(0)
    for r0 in range(0, tc, CONV_TAIL_ROWS):
        out = jnp.dot(act, wpw_ref[...], preferred_element_type=F32) + bpw_ref[...]
        if r0 + CONV_TAIL_ROWS < tc:
            act = activations(r0 + CONV_TAIL_ROWS)
        o_ref[r0:r0 + CONV_TAIL_ROWS, :] = out.astype(BF16)


def _conv_branch(u, wdw, bdw, lng, lnb, wpw, bpw, *, batch, seq, tc):
    t, width = u.shape
    tiles = seq // tc
    cur = pl.BlockSpec((tc, width), lambda b, i: (b * tiles + i, 0))
    prev = pl.BlockSpec((CONV_HALO, width),
                        lambda b, i: (b * (seq // CONV_HALO) + jnp.maximum(i * (tc // CONV_HALO) - 1, 0), 0))
    return pl.pallas_call(
        functools.partial(_conv_kernel, tc=tc),
        grid=(batch, tiles),
        in_specs=[cur, prev, _resident(wdw.shape), _resident(bdw.shape), _resident(lng.shape), _resident(lnb.shape),
                  _resident(wpw.shape), _resident(bpw.shape)],
        out_specs=cur,
        out_shape=jax.ShapeDtypeStruct((t, width), BF16),
        scratch_shapes=[pltpu.VMEM((width // LANES, tc + CONV_HALO, LANES), F32),
                        pltpu.VMEM((width // LANES, tc, LANES), F32)],
        compiler_params=pltpu.CompilerParams(dimension_semantics=("arbitrary", "arbitrary"),
                                             vmem_limit_bytes=VMEM_LIMIT_BYTES),
        name="conformer_conv",
    )(u, u, wdw, bdw, lng, lnb, wpw, bpw)


def _out_proj_kernel(a_ref, c_ref, x_ref, w_ref, b_ref, g_ref, beta_ref, o_ref, *, alpha, splits):
    attn_w = a_ref.shape[1]
    rows = a_ref.shape[0] // splits

    def matmuls(r0):
        mix = jnp.dot(a_ref[r0:r0 + rows, :], w_ref[:attn_w, :], preferred_element_type=F32)
        return mix + jnp.dot(c_ref[r0:r0 + rows, :], w_ref[attn_w:, :], preferred_element_type=F32)

    def norm(r0, mix):
        y = alpha * x_ref[r0:r0 + rows, :] + (mix + b_ref[...])
        o_ref[r0:r0 + rows, :] = _layer_norm(y, g_ref[...], beta_ref[...])

    mix = matmuls(0)
    for k in range(splits):
        nxt = matmuls((k + 1) * rows) if k + 1 < splits else None
        norm(k * rows, mix)
        mix = nxt


def _out_proj(attn, conv, x2, w, b, g, beta, *, alpha, tm, splits):
    t, d = x2.shape
    row = lambda width: pl.BlockSpec((tm, width), lambda i: (i, 0))
    return pl.pallas_call(
        functools.partial(_out_proj_kernel, alpha=alpha, splits=splits),
        grid=(t // tm,),
        in_specs=[row(attn.shape[1]), row(conv.shape[1]), row(d), _resident(w.shape), _resident(b.shape),
                  _resident(g.shape), _resident(beta.shape)],
        out_specs=row(d),
        out_shape=jax.ShapeDtypeStruct((t, d), F32),
        compiler_params=pltpu.CompilerParams(dimension_semantics=("arbitrary",), vmem_limit_bytes=VMEM_LIMIT_BYTES),
        name="out_proj_ln",
    )(attn, conv, x2, w, b, g, beta)


def _ffn_kernel(x_ref, wg_ref, wu_ref, wd_ref, g_ref, beta_ref, o_ref, xb_ref, *, alpha):
    f = pl.program_id(1)

    @pl.when(f == 0)
    def _():
        xb_ref[...] = x_ref[...].astype(BF16)

    @pl.when(f == 0)
    def _():
        o_ref[...] = alpha * x_ref[...]

    xb = xb_ref[...]
    tf = wg_ref.shape[1]
    hidden = []
    for c0 in range(0, tf, tf // 2):
        gate = jnp.dot(xb, wg_ref[:, c0:c0 + tf // 2], preferred_element_type=F32)
        up = jnp.dot(xb, wu_ref[:, c0:c0 + tf // 2], preferred_element_type=F32)
        hidden.append((jax.nn.silu(gate) * up).astype(BF16))
    half = o_ref.shape[1] // 2
    for h, c0 in zip(hidden, range(0, tf, tf // 2)):
        for n0 in (0, half):
            o_ref[:, n0:n0 + half] += jnp.dot(h, wd_ref[c0:c0 + tf // 2, n0:n0 + half], preferred_element_type=F32)

    @pl.when(f == pl.num_programs(1) - 1)
    def _():
        o_ref[...] = _layer_norm(o_ref[...], g_ref[...], beta_ref[...])


def _ffn(x1, wg, wu, wd, g, beta, *, alpha, tm, tf):
    t, d = x1.shape
    nf = wg.shape[1] // tf
    return pl.pallas_call(
        functools.partial(_ffn_kernel, alpha=alpha),
        grid=(t // tm, nf),
        in_specs=[pl.BlockSpec((tm, d), lambda i, f: (i, 0)),
                  pl.BlockSpec((d, tf), lambda i, f: (0, f)),
                  pl.BlockSpec((d, tf), lambda i, f: (0, f)),
                  pl.BlockSpec((tf, d), lambda i, f: (f, 0)),
                  _resident(g.shape), _resident(beta.shape)],
        out_specs=pl.BlockSpec((tm, d), lambda i, f: (i, 0)),
        out_shape=jax.ShapeDtypeStruct((t, d), F32),
        scratch_shapes=[pltpu.VMEM((tm, d), BF16)],
        compiler_params=pltpu.CompilerParams(dimension_semantics=("arbitrary", "arbitrary"),
                                             vmem_limit_bytes=VMEM_LIMIT_BYTES),
        name="swiglu_ffn_ln",
    )(x1, wg, wu, wd, g, beta)


def kernel(x, positions, w_in, b_in, sinks, w_dw, b_dw, conv_ln_g, conv_ln_b, w_pw2, b_pw2, w_out, b_out, ln1_g,
           ln1_b, w_gate, w_up, w_down, ln2_g, ln2_b):
    batch, seq, d = x.shape
    depth = w_in.shape[0]
    alpha = (2 * depth) ** 0.25
    t = batch * seq
    row = lambda a: a.reshape(1, -1).astype(F32)

    half = HEAD_DIM // 2
    inv_freq = 1.0 / (ROPE_THETA ** (jnp.arange(half, dtype=F32) * 2.0 / HEAD_DIM))
    invf = jnp.tile(inv_freq, LANES // half).reshape(1, LANES)
    pos2 = positions.reshape(t, 1)

    x2 = x.reshape(t, d)
    for l in range(depth):
        q, kd, vd, u = _in_proj(x2, pos2, invf, w_in[l].astype(BF16), row(b_in[l]), tm=TOKEN_TILE)
        attn = _attention(sinks[l].astype(F32), q, kd, vd, batch=batch, seq=seq, tq=ATTN_TILE)
        wdw = jnp.repeat(w_dw[l].reshape(CONV_KERNEL, -1).astype(F32), SUBLANES, axis=0)
        conv = _conv_branch(u, wdw, row(b_dw[l]), row(conv_ln_g[l]), row(conv_ln_b[l]), w_pw2[l].astype(BF16),
                            row(b_pw2[l]), batch=batch, seq=seq, tc=CONV_TILE)
        x2 = _out_proj(attn, conv, x2, w_out[l].astype(BF16), row(b_out[l]), row(ln1_g[l]), row(ln1_b[l]),
                       alpha=alpha, tm=TOKEN_TILE, splits=4)
        x2 = _ffn(x2, w_gate[l].astype(BF16), w_up[l].astype(BF16), w_down[l].astype(BF16), row(ln2_g[l]),
                  row(ln2_b[l]), alpha=alpha, tm=TOKEN_TILE, tf=FFN_CHUNK)
    return x2.reshape(batch, seq, d)
```

```python
import functools

import jax
import jax.numpy as jnp
from jax import lax
from jax.experimental import pallas as pl
from jax.experimental.pallas import tpu as pltpu

HEAD_DIM = 64
N_Q_HEADS = 16
N_KV_HEADS = 2
WINDOW = 128
BLOCK = 128
CONV_KERNEL = 31
ROPE_THETA = 10000.0
LN_EPS = 1e-5

LANES = 128
SUBLANES = 8
CONV_HALO = 32
CONV_STRIDE = 4
VMEM_LIMIT_BYTES = 60 * 1024 * 1024

TOKEN_TILE = 1024
ATTN_TILE = 512
CONV_TILE = 1024
CONV_TAIL_ROWS = 256
FFN_CHUNK = 512

F32 = jnp.float32
BF16 = jnp.bfloat16


def _resident(shape):
    return pl.BlockSpec(shape, lambda *_: (0,) * len(shape), pipeline_mode=pl.Buffered(1))


def _layer_norm(y, g, b):
    mu = jnp.mean(y, axis=-1, keepdims=True)
    d = y - mu
    var = jnp.mean(d * d, axis=-1, keepdims=True)
    return d * lax.rsqrt(var + LN_EPS) * g + b


def _in_proj_kernel(x_ref, pos_ref, invf_ref, wt_ref, bt_ref, w_ref, b_ref, qt_ref, k_ref, vt_ref, u_ref, *,
                    attn_w, kv_w):
    xb = x_ref[...].astype(BF16)
    half = HEAD_DIM // 2
    ang = invf_ref[...] * pos_ref[...].astype(F32)
    cos = jnp.cos(ang)
    sin = jnp.sin(ang)

    def rope_t(h):
        x1, x2 = h[:half], h[half:]
        return jnp.concatenate([x1 * cos - x2 * sin, x2 * cos + x1 * sin], axis=0)

    ht = lax.dot_general(wt_ref[...], xb, (((1,), (1,)), ((), ())), preferred_element_type=F32) + bt_ref[...]
    for r0 in range(0, attn_w, HEAD_DIM):
        qt_ref[r0:r0 + HEAD_DIM, :] = (rope_t(ht[r0:r0 + HEAD_DIM]) * (HEAD_DIM ** -0.5)).astype(BF16)
    kt = jnp.concatenate([rope_t(ht[attn_w + r0:attn_w + r0 + HEAD_DIM]) for r0 in range(0, kv_w, HEAD_DIM)], axis=0)
    k_ref[...] = kt.T.astype(BF16)
    vt_ref[...] = ht[attn_w + kv_w:].astype(BF16)

    chunk = 4 * LANES
    conv_w = u_ref.shape[1]
    for c0 in range(0, conv_w, chunk):
        a = jnp.dot(xb, w_ref[:, c0:c0 + chunk], preferred_element_type=F32) + b_ref[:, c0:c0 + chunk]
        g = jnp.dot(xb, w_ref[:, conv_w + c0:conv_w + c0 + chunk], preferred_element_type=F32)
        g = g + b_ref[:, conv_w + c0:conv_w + c0 + chunk]
        u_ref[:, c0:c0 + chunk] = a * jax.nn.sigmoid(g)


def _in_proj(x2, pos_row, invf_col, wt, bt, w, b, *, tm):
    t, d = x2.shape
    attn_w = N_Q_HEADS * HEAD_DIM
    kv_w = N_KV_HEADS * HEAD_DIM
    conv_w = w.shape[1] // 2
    assert wt.shape[0] == attn_w + 2 * kv_w and kv_w == LANES and conv_w % (4 * LANES) == 0
    row = lambda width: pl.BlockSpec((tm, width), lambda i: (i, 0))
    col = lambda height: pl.BlockSpec((height, tm), lambda i: (0, i))
    return pl.pallas_call(
        functools.partial(_in_proj_kernel, attn_w=attn_w, kv_w=kv_w),
        grid=(t // tm,),
        in_specs=[row(d), col(1), _resident(invf_col.shape), _resident(wt.shape), _resident(bt.shape),
                  _resident(w.shape), _resident(b.shape)],
        out_specs=[col(attn_w), row(kv_w), col(kv_w), row(conv_w)],
        out_shape=[jax.ShapeDtypeStruct((attn_w, t), BF16), jax.ShapeDtypeStruct((t, kv_w), BF16),
                   jax.ShapeDtypeStruct((kv_w, t), BF16), jax.ShapeDtypeStruct((t, conv_w), F32)],
        compiler_params=pltpu.CompilerParams(dimension_semantics=("arbitrary",), vmem_limit_bytes=VMEM_LIMIT_BYTES),
        name="in_proj",
    )(x2, pos_row, invf_col, wt, bt, w, b)


def _attn_kernel(sinks_ref, qt_ref, kc_ref, kp_ref, vtc_ref, vtp_ref, o_ref, kfull_ref, vtfull_ref, *, tq):
    i = pl.program_id(1)
    kfull_ref[:BLOCK, :] = kp_ref[...]
    kfull_ref[BLOCK:, :] = kc_ref[...]
    vtfull_ref[:, :BLOCK] = vtp_ref[...]
    vtfull_ref[:, BLOCK:] = vtc_ref[...]

    assert WINDOW == BLOCK
    own = (lax.broadcasted_iota(jnp.int32, (BLOCK, BLOCK), 0) <= lax.broadcasted_iota(jnp.int32, (BLOCK, BLOCK), 1))
    heads_per_kv = N_Q_HEADS // N_KV_HEADS
    zero = jnp.zeros((), BF16)
    no_q = jnp.zeros((HEAD_DIM, 2 * BLOCK), BF16)

    for j in range(tq // BLOCK):
        cols = slice(j * BLOCK, (j + 1) * BLOCK)
        kwin = kfull_ref[j * BLOCK:(j + 2) * BLOCK, :]
        vwin = vtfull_ref[:, j * BLOCK:(j + 2) * BLOCK]

        def scores(pair):
            r0 = 2 * pair * HEAD_DIM
            q2 = jnp.concatenate([qt_ref[r0:r0 + HEAD_DIM, cols], qt_ref[r0 + HEAD_DIM:r0 + 2 * HEAD_DIM, cols]], axis=1)
            rhs = [q2, no_q] if 2 * pair < heads_per_kv else [no_q, q2]
            return jnp.dot(kwin, jnp.concatenate(rhs, axis=0), preferred_element_type=F32)

        s_all = [scores(pair) for pair in range(N_Q_HEADS // 2)]

        def logits(h):
            s = s_all[h // 2][:, (h % 2) * BLOCK:(h % 2 + 1) * BLOCK]
            before = s[:BLOCK]
            if j == 0:
                before = jnp.where(i > 0, before, -jnp.inf)
            return jnp.where(own, s[BLOCK:], before)

        heads = range(N_Q_HEADS)
        col_max = [jnp.maximum(jnp.max(logits(h), axis=0, keepdims=True), sinks_ref[h]) for h in heads]
        expo = [jnp.exp(logits(h) - col_max[h]) for h in heads]
        inv = [1.0 / (jnp.sum(expo[h], axis=0, keepdims=True) + jnp.exp(sinks_ref[h] - col_max[h])) for h in heads]

        def probs_t(h):
            p = (expo[h] * inv[h]).astype(BF16)
            return jnp.concatenate([jnp.where(own, zero, p), jnp.where(own, p, zero)], axis=0)

        for pair in range(N_Q_HEADS // 2):
            g = 2 * pair // heads_per_kv
            pt = jnp.concatenate([probs_t(2 * pair), probs_t(2 * pair + 1)], axis=1)
            ot = jnp.dot(vwin[g * HEAD_DIM:(g + 1) * HEAD_DIM, :], pt, preferred_element_type=F32)
            o2 = jnp.concatenate([ot[:, :BLOCK], ot[:, BLOCK:]], axis=0)
            o_ref[j * BLOCK:(j + 1) * BLOCK, pair * LANES:(pair + 1) * LANES] = o2.T.astype(BF16)


def _attention(sinks, qt, k, vt, *, batch, seq, tq):
    attn_w, t = qt.shape
    kvw = k.shape[1]
    tiles = seq // tq
    prev_block = lambda b, i: b * (seq // BLOCK) + jnp.maximum(i * (tq // BLOCK) - 1, 0)
    return pl.pallas_call(
        functools.partial(_attn_kernel, tq=tq),
        grid=(batch, tiles),
        in_specs=[pl.BlockSpec(memory_space=pltpu.SMEM),
                  pl.BlockSpec((attn_w, tq), lambda b, i: (0, b * tiles + i)),
                  pl.BlockSpec((tq, kvw), lambda b, i: (b * tiles + i, 0)),
                  pl.BlockSpec((BLOCK, kvw), lambda b, i: (prev_block(b, i), 0)),
                  pl.BlockSpec((kvw, tq), lambda b, i: (0, b * tiles + i)),
                  pl.BlockSpec((kvw, BLOCK), lambda b, i: (0, prev_block(b, i)))],
        out_specs=pl.BlockSpec((tq, attn_w), lambda b, i: (b * tiles + i, 0)),
        out_shape=jax.ShapeDtypeStruct((t, attn_w), BF16),
        scratch_shapes=[pltpu.VMEM((tq + BLOCK, kvw), BF16), pltpu.VMEM((kvw, tq + BLOCK), BF16)],
        compiler_params=pltpu.CompilerParams(dimension_semantics=("arbitrary", "arbitrary"),
                                             vmem_limit_bytes=VMEM_LIMIT_BYTES),
        name="swa_attention",
    )(sinks, qt, k, k, vt, vt)


def _conv_kernel(uc_ref, up_ref, wdw_ref, bdw_ref, lng_ref, lnb_ref, wpw_ref, bpw_ref, o_ref, ext_ref, y_ref, *, tc):
    i = pl.program_id(1)
    n_slabs = uc_ref.shape[1] // LANES
    halo = up_ref[...]
    halo = jnp.where(i > 0, halo, jnp.zeros_like(halo))
    for c in range(n_slabs):
        ext_ref[c, :CONV_HALO, :] = halo[:, c * LANES:(c + 1) * LANES]
        ext_ref[c, CONV_HALO:, :] = uc_ref[:, c * LANES:(c + 1) * LANES]

    shift0 = CONV_HALO - (CONV_KERNEL - 1)
    unit = SUBLANES * CONV_STRIDE
    passes = ((0, (CONV_KERNEL + 1) // 2), ((CONV_KERNEL + 1) // 2, CONV_KERNEL))
    for c in range(n_slabs):
        for first, (j0, j1) in zip((True, False), passes):
            w_rows = {j: wdw_ref[SUBLANES * j:SUBLANES * (j + 1), c * LANES:(c + 1) * LANES] for j in range(j0, j1)}
            bias = jnp.broadcast_to(bdw_ref[:, c * LANES:(c + 1) * LANES], (SUBLANES, LANES))

            def unit_body(r, carry, c=c, first=first, j0=j0, j1=j1, w_rows=w_rows, bias=bias):
                base = r * unit
                out = [y_ref.at[c, pl.ds(base + k, SUBLANES, stride=CONV_STRIDE), :] for k in range(CONV_STRIDE)]
                acc = [[bias if first else out[k][...], None] for k in range(CONV_STRIDE)]
                for jk in range(j0, j1 + CONV_STRIDE - 1):
                    v = ext_ref[c, pl.ds(base + shift0 + jk, SUBLANES, stride=CONV_STRIDE), :]
                    for k in range(CONV_STRIDE):
                        j = jk - k
                        if j0 <= j < j1:
                            term = v * w_rows[j]
                            acc[k][j % 2] = term if acc[k][j % 2] is None else acc[k][j % 2] + term
                for k in range(CONV_STRIDE):
                    out[k][...] = acc[k][0] + acc[k][1]
                return carry

            lax.fori_loop(0, tc // unit, unit_body, 0, unroll=4)

    def activations(r0):
        y = jnp.concatenate([y_ref[c, r0:r0 + CONV_TAIL_ROWS, :] for c in range(n_slabs)], axis=1)
        return jax.nn.silu(_layer_norm(y, lng_ref[...], lnb_ref[...])).astype(BF16)

    act = activations(0)
    for r0 in range(0, tc, CONV_TAIL_ROWS):
        out = jnp.dot(act, wpw_ref[...], preferred_element_type=F32) + bpw_ref[...]
        if r0 + CONV_TAIL_ROWS < tc:
            act = activations(r0 + CONV_TAIL_ROWS)
        o_ref[r0:r0 + CONV_TAIL_ROWS, :] = out.astype(BF16)


def _conv_branch(u, wdw, bdw, lng, lnb, wpw, bpw, *, batch, seq, tc):
    t, width = u.shape
    tiles = seq // tc
    cur = pl.BlockSpec((tc, width), lambda b, i: (b * tiles + i, 0))
    prev = pl.BlockSpec((CONV_HALO, width),
                        lambda b, i: (b * (seq // CONV_HALO) + jnp.maximum(i * (tc // CONV_HALO) - 1, 0), 0))
    return pl.pallas_call(
        functools.partial(_conv_kernel, tc=tc),
        grid=(batch, tiles),
        in_specs=[cur, prev, _resident(wdw.shape), _resident(bdw.shape), _resident(lng.shape), _resident(lnb.shape),
                  _resident(wpw.shape), _resident(bpw.shape)],
        out_specs=cur,
        out_shape=jax.ShapeDtypeStruct((t, width), BF16),
        scratch_shapes=[pltpu.VMEM((width // LANES, tc + CONV_HALO, LANES), F32),
                        pltpu.VMEM((width // LANES, tc, LANES), F32)],
        compiler_params=pltpu.CompilerParams(dimension_semantics=("arbitrary", "arbitrary"),
                                             vmem_limit_bytes=VMEM_LIMIT_BYTES),
        name="conformer_conv",
    )(u, u, wdw, bdw, lng, lnb, wpw, bpw)


def _out_proj_kernel(a_ref, c_ref, x_ref, w_ref, b_ref, g_ref, beta_ref, o_ref, *, alpha, splits):
    attn_w = a_ref.shape[1]
    rows = a_ref.shape[0] // splits

    def matmuls(r0):
        mix = jnp.dot(a_ref[r0:r0 + rows, :], w_ref[:attn_w, :], preferred_element_type=F32)
        return mix + jnp.dot(c_ref[r0:r0 + rows, :], w_ref[attn_w:, :], preferred_element_type=F32)

    def norm(r0, mix):
        y = alpha * x_ref[r0:r0 + rows, :] + (mix + b_ref[...])
        o_ref[r0:r0 + rows, :] = _layer_norm(y, g_ref[...], beta_ref[...])

    mix = matmuls(0)
    for k in range(splits):
        nxt = matmuls((k + 1) * rows) if k + 1 < splits else None
        norm(k * rows, mix)
        mix = nxt


def _out_proj(attn, conv, x2, w, b, g, beta, *, alpha, tm, splits):
    t, d = x2.shape
    row = lambda width: pl.BlockSpec((tm, width), lambda i: (i, 0))
    return pl.pallas_call(
        functools.partial(_out_proj_kernel, alpha=alpha, splits=splits),
        grid=(t // tm,),
        in_specs=[row(attn.shape[1]), row(conv.shape[1]), row(d), _resident(w.shape), _resident(b.shape),
                  _resident(g.shape), _resident(beta.shape)],
        out_specs=row(d),
        out_shape=jax.ShapeDtypeStruct((t, d), F32),
        compiler_params=pltpu.CompilerParams(dimension_semantics=("arbitrary",), vmem_limit_bytes=VMEM_LIMIT_BYTES),
        name="out_proj_ln",
    )(attn, conv, x2, w, b, g, beta)


def _ffn_kernel(x_ref, wg_ref, wu_ref, wd_ref, g_ref, beta_ref, o_ref, xb_ref, *, alpha):
    f = pl.program_id(1)

    @pl.when(f == 0)
    def _():
        xb_ref[...] = x_ref[...].astype(BF16)

    @pl.when(f == 0)
    def _():
        o_ref[...] = alpha * x_ref[...]

    xb = xb_ref[...]
    tf = wg_ref.shape[1]
    hidden = []
    for c0 in range(0, tf, tf // 2):
        gate = jnp.dot(xb, wg_ref[:, c0:c0 + tf // 2], preferred_element_type=F32)
        up = jnp.dot(xb, wu_ref[:, c0:c0 + tf // 2], preferred_element_type=F32)
        hidden.append((jax.nn.silu(gate) * up).astype(BF16))
    half = o_ref.shape[1] // 2
    for h, c0 in zip(hidden, range(0, tf, tf // 2)):
        for n0 in (0, half):
            o_ref[:, n0:n0 + half] += jnp.dot(h, wd_ref[c0:c0 + tf // 2, n0:n0 + half], preferred_element_type=F32)

    @pl.when(f == pl.num_programs(1) - 1)
    def _():
        o_ref[...] = _layer_norm(o_ref[...], g_ref[...], beta_ref[...])


def _ffn(x1, wg, wu, wd, g, beta, *, alpha, tm, tf):
    t, d = x1.shape
    nf = wg.shape[1] // tf
    return pl.pallas_call(
        functools.partial(_ffn_kernel, alpha=alpha),
        grid=(t // tm, nf),
        in_specs=[pl.BlockSpec((tm, d), lambda i, f: (i, 0)),
                  pl.BlockSpec((d, tf), lambda i, f: (0, f)),
                  pl.BlockSpec((d, tf), lambda i, f: (0, f)),
                  pl.BlockSpec((tf, d), lambda i, f: (f, 0)),
                  _resident(g.shape), _resident(beta.shape)],
        out_specs=pl.BlockSpec((tm, d), lambda i, f: (i, 0)),
        out_shape=jax.ShapeDtypeStruct((t, d), F32),
        scratch_shapes=[pltpu.VMEM((tm, d), BF16)],
        compiler_params=pltpu.CompilerParams(dimension_semantics=("arbitrary", "arbitrary"),
                                             vmem_limit_bytes=VMEM_LIMIT_BYTES),
        name="swiglu_ffn_ln",
    )(x1, wg, wu, wd, g, beta)


def kernel(x, positions, w_in, b_in, sinks, w_dw, b_dw, conv_ln_g, conv_ln_b, w_pw2, b_pw2, w_out, b_out, ln1_g,
           ln1_b, w_gate, w_up, w_down, ln2_g, ln2_b):
    batch, seq, d = x.shape
    depth = w_in.shape[0]
    alpha = (2 * depth) ** 0.25
    t = batch * seq
    row = lambda a: a.reshape(1, -1).astype(F32)

    half = HEAD_DIM // 2
    inv_freq = 1.0 / (ROPE_THETA ** (jnp.arange(half, dtype=F32) * 2.0 / HEAD_DIM))
    invf_col = inv_freq.reshape(half, 1)
    pos_row = positions.reshape(1, t)
    n_qkv = (N_Q_HEADS + 2 * N_KV_HEADS) * HEAD_DIM

    x2 = x.reshape(t, d)
    for l in range(depth):
        qt, k, vt, u = _in_proj(x2, pos_row, invf_col, w_in[l][:, :n_qkv].T.astype(BF16),
                                b_in[l][:n_qkv].reshape(-1, 1).astype(F32), w_in[l][:, n_qkv:].astype(BF16),
                                row(b_in[l][n_qkv:]), tm=TOKEN_TILE)
        attn = _attention(sinks[l].astype(F32), qt, k, vt, batch=batch, seq=seq, tq=ATTN_TILE)
        wdw = jnp.repeat(w_dw[l].reshape(CONV_KERNEL, -1).astype(F32), SUBLANES, axis=0)
        conv = _conv_branch(u, wdw, row(b_dw[l]), row(conv_ln_g[l]), row(conv_ln_b[l]), w_pw2[l].astype(BF16),
                            row(b_pw2[l]), batch=batch, seq=seq, tc=CONV_TILE)
        x2 = _out_proj(attn, conv, x2, w_out[l].astype(BF16), row(b_out[l]), row(ln1_g[l]), row(ln1_b[l]),
                       alpha=alpha, tm=TOKEN_TILE, splits=4)
        x2 = _ffn(x2, w_gate[l].astype(BF16), w_up[l].astype(BF16), w_down[l].astype(BF16), row(ln2_g[l]),
                  row(ln2_b[l]), alpha=alpha, tm=TOKEN_TILE, tf=FFN_CHUNK)
    return x2.reshape(batch, seq, d)
```

```python
import functools

import jax
import jax.numpy as jnp
from jax import lax
from jax.experimental import pallas as pl
from jax.experimental.pallas import tpu as pltpu

HEAD_DIM = 64
N_Q_HEADS = 16
N_KV_HEADS = 2
WINDOW = 128
BLOCK = 128
CONV_KERNEL = 31
ROPE_THETA = 10000.0
LN_EPS = 1e-5

LANES = 128
SUBLANES = 8
CONV_HALO = 32
CONV_STRIDE = 4
VMEM_LIMIT_BYTES = 60 * 1024 * 1024

TOKEN_TILE = 1024
ATTN_TILE = 1024
CONV_TILE = 1024
CONV_TAIL_ROWS = 256
FFN_CHUNK = 512

F32 = jnp.float32
BF16 = jnp.bfloat16


def _resident(shape):
    return pl.BlockSpec(shape, lambda *_: (0,) * len(shape), pipeline_mode=pl.Buffered(1))


def _layer_norm(y, g, b):
    mu = jnp.mean(y, axis=-1, keepdims=True)
    d = y - mu
    var = jnp.mean(d * d, axis=-1, keepdims=True)
    return d * lax.rsqrt(var + LN_EPS) * g + b


def _in_proj_kernel(x_ref, pos_ref, invf_ref, wt_ref, bt_ref, w_ref, b_ref, qt_ref, k_ref, vt_ref, u_ref, *,
                    attn_w, kv_w):
    xb = x_ref[...].astype(BF16)
    half = HEAD_DIM // 2
    ang = invf_ref[...] * pos_ref[...].astype(F32)
    cos = jnp.cos(ang)
    sin = jnp.sin(ang)

    def rope_t(h):
        x1, x2 = h[:half], h[half:]
        return jnp.concatenate([x1 * cos - x2 * sin, x2 * cos + x1 * sin], axis=0)

    ht = lax.dot_general(wt_ref[...], xb, (((1,), (1,)), ((), ())), preferred_element_type=F32) + bt_ref[...]
    for r0 in range(0, attn_w, HEAD_DIM):
        qt_ref[r0:r0 + HEAD_DIM, :] = (rope_t(ht[r0:r0 + HEAD_DIM]) * (HEAD_DIM ** -0.5)).astype(BF16)
    kt = jnp.concatenate([rope_t(ht[attn_w + r0:attn_w + r0 + HEAD_DIM]) for r0 in range(0, kv_w, HEAD_DIM)], axis=0)
    k_ref[...] = kt.T.astype(BF16)
    vt_ref[...] = ht[attn_w + kv_w:].astype(BF16)

    chunk = 4 * LANES
    conv_w = u_ref.shape[1]
    for c0 in range(0, conv_w, chunk):
        a = jnp.dot(xb, w_ref[:, c0:c0 + chunk], preferred_element_type=F32) + b_ref[:, c0:c0 + chunk]
        g = jnp.dot(xb, w_ref[:, conv_w + c0:conv_w + c0 + chunk], preferred_element_type=F32)
        g = g + b_ref[:, conv_w + c0:conv_w + c0 + chunk]
        u_ref[:, c0:c0 + chunk] = a * jax.nn.sigmoid(g)


def _in_proj(x2, pos_row, invf_col, wt, bt, w, b, *, tm):
    t, d = x2.shape
    attn_w = N_Q_HEADS * HEAD_DIM
    kv_w = N_KV_HEADS * HEAD_DIM
    conv_w = w.shape[1] // 2
    assert wt.shape[0] == attn_w + 2 * kv_w and kv_w == LANES and conv_w % (4 * LANES) == 0
    row = lambda width: pl.BlockSpec((tm, width), lambda i: (i, 0))
    col = lambda height: pl.BlockSpec((height, tm), lambda i: (0, i))
    return pl.pallas_call(
        functools.partial(_in_proj_kernel, attn_w=attn_w, kv_w=kv_w),
        grid=(t // tm,),
        in_specs=[row(d), col(1), _resident(invf_col.shape), _resident(wt.shape), _resident(bt.shape),
                  _resident(w.shape), _resident(b.shape)],
        out_specs=[col(attn_w), row(kv_w), col(kv_w), row(conv_w)],
        out_shape=[jax.ShapeDtypeStruct((attn_w, t), BF16), jax.ShapeDtypeStruct((t, kv_w), BF16),
                   jax.ShapeDtypeStruct((kv_w, t), BF16), jax.ShapeDtypeStruct((t, conv_w), F32)],
        compiler_params=pltpu.CompilerParams(dimension_semantics=("arbitrary",), vmem_limit_bytes=VMEM_LIMIT_BYTES),
        name="in_proj",
    )(x2, pos_row, invf_col, wt, bt, w, b)


def _attn_kernel(sinks_ref, qt_ref, kc_ref, kp_ref, vtc_ref, vtp_ref, *rest, tq, n_cast):
    cast_in, o_ref, cast_out = rest[:n_cast], rest[n_cast], rest[n_cast + 1:2 * n_cast + 1]
    kfull_ref, vtfull_ref = rest[2 * n_cast + 1:]
    i = pl.program_id(1)
    kfull_ref[:BLOCK, :] = kp_ref[...]
    kfull_ref[BLOCK:, :] = kc_ref[...]
    vtfull_ref[:, :BLOCK] = vtp_ref[...]
    vtfull_ref[:, BLOCK:] = vtc_ref[...]

    assert WINDOW == BLOCK
    own = (lax.broadcasted_iota(jnp.int32, (BLOCK, BLOCK), 0) <= lax.broadcasted_iota(jnp.int32, (BLOCK, BLOCK), 1))
    heads_per_kv = N_Q_HEADS // N_KV_HEADS
    zero = jnp.zeros((), BF16)
    no_q = jnp.zeros((HEAD_DIM, 2 * BLOCK), BF16)

    for j in range(tq // BLOCK):
        cols = slice(j * BLOCK, (j + 1) * BLOCK)
        kwin = kfull_ref[j * BLOCK:(j + 2) * BLOCK, :]
        vwin = vtfull_ref[:, j * BLOCK:(j + 2) * BLOCK]

        def scores(pair):
            r0 = 2 * pair * HEAD_DIM
            q2 = jnp.concatenate([qt_ref[r0:r0 + HEAD_DIM, cols], qt_ref[r0 + HEAD_DIM:r0 + 2 * HEAD_DIM, cols]], axis=1)
            rhs = [q2, no_q] if 2 * pair < heads_per_kv else [no_q, q2]
            return jnp.dot(kwin, jnp.concatenate(rhs, axis=0), preferred_element_type=F32)

        s_all = [scores(pair) for pair in range(N_Q_HEADS // 2)]

        def logits(h):
            s = s_all[h // 2][:, (h % 2) * BLOCK:(h % 2 + 1) * BLOCK]
            before = s[:BLOCK]
            if j == 0:
                before = jnp.where(i > 0, before, -jnp.inf)
            return jnp.where(own, s[BLOCK:], before)

        heads = range(N_Q_HEADS)
        col_max = [jnp.maximum(jnp.max(logits(h), axis=0, keepdims=True), sinks_ref[h]) for h in heads]
        expo = [jnp.exp(logits(h) - col_max[h]) for h in heads]
        inv = [1.0 / (jnp.sum(expo[h], axis=0, keepdims=True) + jnp.exp(sinks_ref[h] - col_max[h])) for h in heads]

        def probs_t(h):
            p = (expo[h] * inv[h]).astype(BF16)
            return jnp.concatenate([jnp.where(own, zero, p), jnp.where(own, p, zero)], axis=0)

        for pair in range(N_Q_HEADS // 2):
            g = 2 * pair // heads_per_kv
            pt = jnp.concatenate([probs_t(2 * pair), probs_t(2 * pair + 1)], axis=1)
            ot = jnp.dot(vwin[g * HEAD_DIM:(g + 1) * HEAD_DIM, :], pt, preferred_element_type=F32)
            o2 = jnp.concatenate([ot[:, :BLOCK], ot[:, BLOCK:]], axis=0)
            o_ref[j * BLOCK:(j + 1) * BLOCK, pair * LANES:(pair + 1) * LANES] = o2.T.astype(BF16)

    for src, dst in zip(cast_in, cast_out):
        dst[...] = src[...].astype(BF16)


def _attention(sinks, qt, k, vt, to_bf16, *, batch, seq, tq):
    attn_w, t = qt.shape
    kvw = k.shape[1]
    tiles = seq // tq
    steps = batch * tiles
    prev_block = lambda b, i: b * (seq // BLOCK) + jnp.maximum(i * (tq // BLOCK) - 1, 0)
    assert all(w.shape[0] % (steps * 2 * SUBLANES) == 0 for w in to_bf16)
    slab = lambda w: pl.BlockSpec((w.shape[0] // steps, w.shape[1]), lambda b, i: (b * tiles + i, 0))
    outs = pl.pallas_call(
        functools.partial(_attn_kernel, tq=tq, n_cast=len(to_bf16)),
        grid=(batch, tiles),
        in_specs=[pl.BlockSpec(memory_space=pltpu.SMEM),
                  pl.BlockSpec((attn_w, tq), lambda b, i: (0, b * tiles + i)),
                  pl.BlockSpec((tq, kvw), lambda b, i: (b * tiles + i, 0)),
                  pl.BlockSpec((BLOCK, kvw), lambda b, i: (prev_block(b, i), 0)),
                  pl.BlockSpec((kvw, tq), lambda b, i: (0, b * tiles + i)),
                  pl.BlockSpec((kvw, BLOCK), lambda b, i: (0, prev_block(b, i)))] + [slab(w) for w in to_bf16],
        out_specs=[pl.BlockSpec((tq, attn_w), lambda b, i: (b * tiles + i, 0))] + [slab(w) for w in to_bf16],
        out_shape=[jax.ShapeDtypeStruct((t, attn_w), BF16)] + [jax.ShapeDtypeStruct(w.shape, BF16) for w in to_bf16],
        scratch_shapes=[pltpu.VMEM((tq + BLOCK, kvw), BF16), pltpu.VMEM((kvw, tq + BLOCK), BF16)],
        compiler_params=pltpu.CompilerParams(dimension_semantics=("arbitrary", "arbitrary"),
                                             vmem_limit_bytes=VMEM_LIMIT_BYTES),
        name="swa_attention",
    )(sinks, qt, k, k, vt, vt, *to_bf16)
    return outs[0], outs[1:]


def _conv_kernel(uc_ref, up_ref, wdw_ref, bdw_ref, lng_ref, lnb_ref, wpw_ref, bpw_ref, o_ref, ext_ref, y_ref, *, tc):
    i = pl.program_id(1)
    n_slabs = uc_ref.shape[1] // LANES
    halo = up_ref[...]
    halo = jnp.where(i > 0, halo, jnp.zeros_like(halo))
    for c in range(n_slabs):
        ext_ref[c, :CONV_HALO, :] = halo[:, c * LANES:(c + 1) * LANES]
        ext_ref[c, CONV_HALO:, :] = uc_ref[:, c * LANES:(c + 1) * LANES]

    shift0 = CONV_HALO - (CONV_KERNEL - 1)
    unit = SUBLANES * CONV_STRIDE
    passes = ((0, (CONV_KERNEL + 1) // 2), ((CONV_KERNEL + 1) // 2, CONV_KERNEL))
    for c in range(n_slabs):
        for first, (j0, j1) in zip((True, False), passes):
            w_rows = {j: wdw_ref[SUBLANES * j:SUBLANES * (j + 1), c * LANES:(c + 1) * LANES] for j in range(j0, j1)}
            bias = jnp.broadcast_to(bdw_ref[:, c * LANES:(c + 1) * LANES], (SUBLANES, LANES))

            def unit_body(r, carry, c=c, first=first, j0=j0, j1=j1, w_rows=w_rows, bias=bias):
                base = r * unit
                out = [y_ref.at[c, pl.ds(base + k, SUBLANES, stride=CONV_STRIDE), :] for k in range(CONV_STRIDE)]
                acc = [[bias if first else out[k][...], None] for k in range(CONV_STRIDE)]
                for jk in range(j0, j1 + CONV_STRIDE - 1):
                    v = ext_ref[c, pl.ds(base + shift0 + jk, SUBLANES, stride=CONV_STRIDE), :]
                    for k in range(CONV_STRIDE):
                        j = jk - k
                        if j0 <= j < j1:
                            term = v * w_rows[j]
                            acc[k][j % 2] = term if acc[k][j % 2] is None else acc[k][j % 2] + term
                for k in range(CONV_STRIDE):
                    out[k][...] = acc[k][0] + acc[k][1]
                return carry

            lax.fori_loop(0, tc // unit, unit_body, 0, unroll=4)

    def activations(r0):
        y = jnp.concatenate([y_ref[c, r0:r0 + CONV_TAIL_ROWS, :] for c in range(n_slabs)], axis=1)
        return jax.nn.silu(_layer_norm(y, lng_ref[...], lnb_ref[...])).astype(BF16)

    act = activations(0)
    for r0 in range(0, tc, CONV_TAIL_ROWS):
        out = jnp.dot(act, wpw_ref[...], preferred_element_type=F32) + bpw_ref[...]
        if r0 + CONV_TAIL_ROWS < tc:
            act = activations(r0 + CONV_TAIL_ROWS)
        o_ref[r0:r0 + CONV_TAIL_ROWS, :] = out.astype(BF16)


def _conv_branch(u, wdw, bdw, lng, lnb, wpw, bpw, *, batch, seq, tc):
    t, width = u.shape
    tiles = seq // tc
    cur = pl.BlockSpec((tc, width), lambda b, i: (b * tiles + i, 0))
    prev = pl.BlockSpec((CONV_HALO, width),
                        lambda b, i: (b * (seq // CONV_HALO) + jnp.maximum(i * (tc // CONV_HALO) - 1, 0), 0))
    return pl.pallas_call(
        functools.partial(_conv_kernel, tc=tc),
        grid=(batch, tiles),
        in_specs=[cur, prev, _resident(wdw.shape), _resident(bdw.shape), _resident(lng.shape), _resident(lnb.shape),
                  _resident(wpw.shape), _resident(bpw.shape)],
        out_specs=cur,
        out_shape=jax.ShapeDtypeStruct((t, width), BF16),
        scratch_shapes=[pltpu.VMEM((width // LANES, tc + CONV_HALO, LANES), F32),
                        pltpu.VMEM((width // LANES, tc, LANES), F32)],
        compiler_params=pltpu.CompilerParams(dimension_semantics=("arbitrary", "arbitrary"),
                                             vmem_limit_bytes=VMEM_LIMIT_BYTES),
        name="conformer_conv",
    )(u, u, wdw, bdw, lng, lnb, wpw, bpw)


def _out_proj_kernel(a_ref, c_ref, x_ref, w_ref, b_ref, g_ref, beta_ref, o_ref, *, alpha, splits):
    attn_w = a_ref.shape[1]
    rows = a_ref.shape[0] // splits

    def matmuls(r0):
        mix = jnp.dot(a_ref[r0:r0 + rows, :], w_ref[:attn_w, :], preferred_element_type=F32)
        return mix + jnp.dot(c_ref[r0:r0 + rows, :], w_ref[attn_w:, :], preferred_element_type=F32)

    def norm(r0, mix):
        y = alpha * x_ref[r0:r0 + rows, :] + (mix + b_ref[...])
        o_ref[r0:r0 + rows, :] = _layer_norm(y, g_ref[...], beta_ref[...])

    mix = matmuls(0)
    for k in range(splits):
        nxt = matmuls((k + 1) * rows) if k + 1 < splits else None
        norm(k * rows, mix)
        mix = nxt


def _out_proj(attn, conv, x2, w, b, g, beta, *, alpha, tm, splits):
    t, d = x2.shape
    row = lambda width: pl.BlockSpec((tm, width), lambda i: (i, 0))
    return pl.pallas_call(
        functools.partial(_out_proj_kernel, alpha=alpha, splits=splits),
        grid=(t // tm,),
        in_specs=[row(attn.shape[1]), row(conv.shape[1]), row(d), _resident(w.shape), _resident(b.shape),
                  _resident(g.shape), _resident(beta.shape)],
        out_specs=row(d),
        out_shape=jax.ShapeDtypeStruct((t, d), F32),
        compiler_params=pltpu.CompilerParams(dimension_semantics=("arbitrary",), vmem_limit_bytes=VMEM_LIMIT_BYTES),
        name="out_proj_ln",
    )(attn, conv, x2, w, b, g, beta)


def _ffn_kernel(x_ref, wg_ref, wu_ref, wd_ref, g_ref, beta_ref, o_ref, xb_ref, *, alpha):
    f = pl.program_id(1)

    @pl.when(f == 0)
    def _():
        xb_ref[...] = x_ref[...].astype(BF16)

    @pl.when(f == 0)
    def _():
        o_ref[...] = alpha * x_ref[...]

    xb = xb_ref[...]
    tf = wg_ref.shape[1]
    hidden = []
    for c0 in range(0, tf, tf // 2):
        gate = jnp.dot(xb, wg_ref[:, c0:c0 + tf // 2], preferred_element_type=F32)
        up = jnp.dot(xb, wu_ref[:, c0:c0 + tf // 2], preferred_element_type=F32)
        hidden.append((jax.nn.silu(gate) * up).astype(BF16))
    half = o_ref.shape[1] // 2
    for h, c0 in zip(hidden, range(0, tf, tf // 2)):
        for n0 in (0, half):
            o_ref[:, n0:n0 + half] += jnp.dot(h, wd_ref[c0:c0 + tf // 2, n0:n0 + half], preferred_element_type=F32)

    @pl.when(f == pl.num_programs(1) - 1)
    def _():
        o_ref[...] = _layer_norm(o_ref[...], g_ref[...], beta_ref[...])


def _ffn(x1, wg, wu, wd, g, beta, *, alpha, tm, tf):
    t, d = x1.shape
    nf = wg.shape[1] // tf
    return pl.pallas_call(
        functools.partial(_ffn_kernel, alpha=alpha),
        grid=(t // tm, nf),
        in_specs=[pl.BlockSpec((tm, d), lambda i, f: (i, 0)),
                  pl.BlockSpec((d, tf), lambda i, f: (0, f)),
                  pl.BlockSpec((d, tf), lambda i, f: (0, f)),
                  pl.BlockSpec((tf, d), lambda i, f: (f, 0)),
                  _resident(g.shape), _resident(beta.shape)],
        out_specs=pl.BlockSpec((tm, d), lambda i, f: (i, 0)),
        out_shape=jax.ShapeDtypeStruct((t, d), F32),
        scratch_shapes=[pltpu.VMEM((tm, d), BF16)],
        compiler_params=pltpu.CompilerParams(dimension_semantics=("arbitrary", "arbitrary"),
                                             vmem_limit_bytes=VMEM_LIMIT_BYTES),
        name="swiglu_ffn_ln",
    )(x1, wg, wu, wd, g, beta)


def kernel(x, positions, w_in, b_in, sinks, w_dw, b_dw, conv_ln_g, conv_ln_b, w_pw2, b_pw2, w_out, b_out, ln1_g,
           ln1_b, w_gate, w_up, w_down, ln2_g, ln2_b):
    batch, seq, d = x.shape
    depth = w_in.shape[0]
    alpha = (2 * depth) ** 0.25
    t = batch * seq
    row = lambda a: a.reshape(1, -1).astype(F32)

    half = HEAD_DIM // 2
    inv_freq = 1.0 / (ROPE_THETA ** (jnp.arange(half, dtype=F32) * 2.0 / HEAD_DIM))
    invf_col = inv_freq.reshape(half, 1)
    pos_row = positions.reshape(1, t)
    n_qkv = (N_Q_HEADS + 2 * N_KV_HEADS) * HEAD_DIM

    x2 = x.reshape(t, d)
    for l in range(depth):
        qt, k, vt, u = _in_proj(x2, pos_row, invf_col, w_in[l][:, :n_qkv].T.astype(BF16),
                                b_in[l][:n_qkv].reshape(-1, 1).astype(F32), w_in[l][:, n_qkv:].astype(BF16),
                                row(b_in[l][n_qkv:]), tm=TOKEN_TILE)
        later_weights = [w.astype(F32) for w in (w_pw2[l], w_out[l], w_gate[l], w_up[l], w_down[l])]
        attn, (wpw, wout, wg, wu, wd) = _attention(sinks[l].astype(F32), qt, k, vt, later_weights, batch=batch,
                                                   seq=seq, tq=ATTN_TILE)
        wdw = jnp.repeat(w_dw[l].reshape(CONV_KERNEL, -1).astype(F32), SUBLANES, axis=0)
        conv = _conv_branch(u, wdw, row(b_dw[l]), row(conv_ln_g[l]), row(conv_ln_b[l]), wpw, row(b_pw2[l]),
                            batch=batch, seq=seq, tc=CONV_TILE)
        x2 = _out_proj(attn, conv, x2, wout, row(b_out[l]), row(ln1_g[l]), row(ln1_b[l]), alpha=alpha,
                       tm=TOKEN_TILE, splits=4)
        x2 = _ffn(x2, wg, wu, wd, row(ln2_g[l]), row(ln2_b[l]), alpha=alpha, tm=TOKEN_TILE, tf=FFN_CHUNK)
    return x2.reshape(batch, seq, d)
```

```python
import functools

import jax
import jax.numpy as jnp
from jax import lax
from jax.experimental import pallas as pl
from jax.experimental.pallas import tpu as pltpu

HEAD_DIM = 64
N_Q_HEADS = 16
N_KV_HEADS = 2
WINDOW = 128
BLOCK = 128
CONV_KERNEL = 31
ROPE_THETA = 10000.0
LN_EPS = 1e-5

LANES = 128
SUBLANES = 8
CONV_HALO = 32
CONV_STRIDE = 4
VMEM_LIMIT_BYTES = 60 * 1024 * 1024

TOKEN_TILE = 1024
ATTN_TILE = 1024
CONV_TILE = 1024
CONV_TAIL_ROWS = 256
FFN_CHUNK = 512

F32 = jnp.float32
BF16 = jnp.bfloat16


def _resident(shape):
    return pl.BlockSpec(shape, lambda *_: (0,) * len(shape), pipeline_mode=pl.Buffered(1))


def _layer_norm(y, g, b):
    mu = jnp.mean(y, axis=-1, keepdims=True)
    d = y - mu
    var = jnp.mean(d * d, axis=-1, keepdims=True)
    return d * lax.rsqrt(var + LN_EPS) * g + b


def _in_proj_kernel(x_ref, pos_ref, invf_ref, wt_ref, bt_ref, w_ref, b_ref, qt_ref, k_ref, vt_ref, u_ref, *,
                    attn_w, kv_w):
    xb = x_ref[...].astype(BF16)
    half = HEAD_DIM // 2
    ang = invf_ref[...] * pos_ref[...].astype(F32)
    cos = jnp.cos(ang)
    sin = jnp.sin(ang)

    def rope_t(h):
        x1, x2 = h[:half], h[half:]
        return jnp.concatenate([x1 * cos - x2 * sin, x2 * cos + x1 * sin], axis=0)

    ht = lax.dot_general(wt_ref[...], xb, (((1,), (1,)), ((), ())), preferred_element_type=F32) + bt_ref[...]
    for r0 in range(0, attn_w, HEAD_DIM):
        qt_ref[r0:r0 + HEAD_DIM, :] = (rope_t(ht[r0:r0 + HEAD_DIM]) * (HEAD_DIM ** -0.5)).astype(BF16)
    kt = jnp.concatenate([rope_t(ht[attn_w + r0:attn_w + r0 + HEAD_DIM]) for r0 in range(0, kv_w, HEAD_DIM)], axis=0)
    k_ref[...] = kt.T.astype(BF16)
    vt_ref[...] = ht[attn_w + kv_w:].astype(BF16)

    chunk = 4 * LANES
    conv_w = u_ref.shape[0] * LANES
    for c0 in range(0, conv_w, chunk):
        a = jnp.dot(xb, w_ref[:, c0:c0 + chunk], preferred_element_type=F32) + b_ref[:, c0:c0 + chunk]
        g = jnp.dot(xb, w_ref[:, conv_w + c0:conv_w + c0 + chunk], preferred_element_type=F32)
        g = g + b_ref[:, conv_w + c0:conv_w + c0 + chunk]
        u = a * jax.nn.sigmoid(g)
        for j in range(chunk // LANES):
            u_ref[c0 // LANES + j] = u[:, j * LANES:(j + 1) * LANES]


def _in_proj(x2, pos_row, invf_col, wt, bt, w, b, *, tm):
    t, d = x2.shape
    attn_w = N_Q_HEADS * HEAD_DIM
    kv_w = N_KV_HEADS * HEAD_DIM
    conv_w = w.shape[1] // 2
    assert wt.shape[0] == attn_w + 2 * kv_w and kv_w == LANES and conv_w % (4 * LANES) == 0
    row = lambda width: pl.BlockSpec((tm, width), lambda i: (i, 0))
    col = lambda height: pl.BlockSpec((height, tm), lambda i: (0, i))
    return pl.pallas_call(
        functools.partial(_in_proj_kernel, attn_w=attn_w, kv_w=kv_w),
        grid=(t // tm,),
        in_specs=[row(d), col(1), _resident(invf_col.shape), _resident(wt.shape), _resident(bt.shape),
                  _resident(w.shape), _resident(b.shape)],
        out_specs=[col(attn_w), row(kv_w), col(kv_w), pl.BlockSpec((conv_w // LANES, tm, LANES), lambda i: (0, i, 0))],
        out_shape=[jax.ShapeDtypeStruct((attn_w, t), BF16), jax.ShapeDtypeStruct((t, kv_w), BF16),
                   jax.ShapeDtypeStruct((kv_w, t), BF16), jax.ShapeDtypeStruct((conv_w // LANES, t, LANES), F32)],
        compiler_params=pltpu.CompilerParams(dimension_semantics=("arbitrary",), vmem_limit_bytes=VMEM_LIMIT_BYTES),
        name="in_proj",
    )(x2, pos_row, invf_col, wt, bt, w, b)


def _attn_kernel(sinks_ref, qt_ref, kc_ref, kp_ref, vtc_ref, vtp_ref, *rest, tq, n_cast):
    cast_in, o_ref, cast_out = rest[:n_cast], rest[n_cast], rest[n_cast + 1:2 * n_cast + 1]
    kfull_ref, vtfull_ref = rest[2 * n_cast + 1:]
    i = pl.program_id(1)
    kfull_ref[:BLOCK, :] = kp_ref[...]
    kfull_ref[BLOCK:, :] = kc_ref[...]
    vtfull_ref[:, :BLOCK] = vtp_ref[...]
    vtfull_ref[:, BLOCK:] = vtc_ref[...]

    assert WINDOW == BLOCK
    own = (lax.broadcasted_iota(jnp.int32, (BLOCK, BLOCK), 0) <= lax.broadcasted_iota(jnp.int32, (BLOCK, BLOCK), 1))
    heads_per_kv = N_Q_HEADS // N_KV_HEADS
    zero = jnp.zeros((), BF16)
    no_q = jnp.zeros((HEAD_DIM, 2 * BLOCK), BF16)

    def block_scores(j):
        cols = slice(j * BLOCK, (j + 1) * BLOCK)
        kwin = kfull_ref[j * BLOCK:(j + 2) * BLOCK, :]

        def scores(pair):
            r0 = 2 * pair * HEAD_DIM
            q2 = jnp.concatenate([qt_ref[r0:r0 + HEAD_DIM, cols], qt_ref[r0 + HEAD_DIM:r0 + 2 * HEAD_DIM, cols]], axis=1)
            rhs = [q2, no_q] if 2 * pair < heads_per_kv else [no_q, q2]
            return jnp.dot(kwin, jnp.concatenate(rhs, axis=0), preferred_element_type=F32)

        return [scores(pair) for pair in range(N_Q_HEADS // 2)]

    s_next = block_scores(0)
    for j in range(tq // BLOCK):
        s_all = s_next
        if (j + 1) * BLOCK < tq:
            s_next = block_scores(j + 1)
        vwin = vtfull_ref[:, j * BLOCK:(j + 2) * BLOCK]

        def logits(h):
            s = s_all[h // 2][:, (h % 2) * BLOCK:(h % 2 + 1) * BLOCK]
            before = s[:BLOCK]
            if j == 0:
                before = jnp.where(i > 0, before, -jnp.inf)
            return jnp.where(own, s[BLOCK:], before)

        heads = range(N_Q_HEADS)
        col_max = [jnp.maximum(jnp.max(logits(h), axis=0, keepdims=True), sinks_ref[h]) for h in heads]
        expo = [jnp.exp(logits(h) - col_max[h]) for h in heads]
        inv = [1.0 / (jnp.sum(expo[h], axis=0, keepdims=True) + jnp.exp(sinks_ref[h] - col_max[h])) for h in heads]

        def probs_t(h):
            p = (expo[h] * inv[h]).astype(BF16)
            return jnp.concatenate([jnp.where(own, zero, p), jnp.where(own, p, zero)], axis=0)

        for pair in range(N_Q_HEADS // 2):
            g = 2 * pair // heads_per_kv
            pt = jnp.concatenate([probs_t(2 * pair), probs_t(2 * pair + 1)], axis=1)
            ot = jnp.dot(vwin[g * HEAD_DIM:(g + 1) * HEAD_DIM, :], pt, preferred_element_type=F32)
            o2 = jnp.concatenate([ot[:, :BLOCK], ot[:, BLOCK:]], axis=0)
            o_ref[j * BLOCK:(j + 1) * BLOCK, pair * LANES:(pair + 1) * LANES] = o2.T.astype(BF16)

    for src, dst in zip(cast_in, cast_out):
        dst[...] = src[...].astype(BF16)


def _attention(sinks, qt, k, vt, to_bf16, *, batch, seq, tq):
    attn_w, t = qt.shape
    kvw = k.shape[1]
    tiles = seq // tq
    steps = batch * tiles
    prev_block = lambda b, i: b * (seq // BLOCK) + jnp.maximum(i * (tq // BLOCK) - 1, 0)
    assert all(w.shape[0] % (steps * 2 * SUBLANES) == 0 for w in to_bf16)
    slab = lambda w: pl.BlockSpec((w.shape[0] // steps, w.shape[1]), lambda b, i: (b * tiles + i, 0))
    outs = pl.pallas_call(
        functools.partial(_attn_kernel, tq=tq, n_cast=len(to_bf16)),
        grid=(batch, tiles),
        in_specs=[pl.BlockSpec(memory_space=pltpu.SMEM),
                  pl.BlockSpec((attn_w, tq), lambda b, i: (0, b * tiles + i)),
                  pl.BlockSpec((tq, kvw), lambda b, i: (b * tiles + i, 0)),
                  pl.BlockSpec((BLOCK, kvw), lambda b, i: (prev_block(b, i), 0)),
                  pl.BlockSpec((kvw, tq), lambda b, i: (0, b * tiles + i)),
                  pl.BlockSpec((kvw, BLOCK), lambda b, i: (0, prev_block(b, i)))] + [slab(w) for w in to_bf16],
        out_specs=[pl.BlockSpec((tq, attn_w), lambda b, i: (b * tiles + i, 0))] + [slab(w) for w in to_bf16],
        out_shape=[jax.ShapeDtypeStruct((t, attn_w), BF16)] + [jax.ShapeDtypeStruct(w.shape, BF16) for w in to_bf16],
        scratch_shapes=[pltpu.VMEM((tq + BLOCK, kvw), BF16), pltpu.VMEM((kvw, tq + BLOCK), BF16)],
        compiler_params=pltpu.CompilerParams(dimension_semantics=("arbitrary", "arbitrary"),
                                             vmem_limit_bytes=VMEM_LIMIT_BYTES),
        name="swa_attention",
    )(sinks, qt, k, k, vt, vt, *to_bf16)
    return outs[0], outs[1:]


def _conv_kernel(uc_ref, up_ref, wdw_ref, bdw_ref, lng_ref, lnb_ref, wpw_ref, bpw_ref, o_ref, head_ref, y_ref, *, tc):
    i = pl.program_id(1)
    n_slabs = uc_ref.shape[0]
    unit = SUBLANES * CONV_STRIDE
    assert unit == CONV_HALO
    halo = up_ref[...]
    head_ref[:, :CONV_HALO, :] = jnp.where(i > 0, halo, jnp.zeros_like(halo))
    head_ref[:, CONV_HALO:, :] = uc_ref[:, :unit, :]

    shift0 = CONV_HALO - (CONV_KERNEL - 1)
    passes = ((0, (CONV_KERNEL + 1) // 2), ((CONV_KERNEL + 1) // 2, CONV_KERNEL))
    for c in range(n_slabs):
        for first, (j0, j1) in zip((True, False), passes):
            w_rows = {j: wdw_ref[SUBLANES * j:SUBLANES * (j + 1), c * LANES:(c + 1) * LANES] for j in range(j0, j1)}
            bias = jnp.broadcast_to(bdw_ref[:, c * LANES:(c + 1) * LANES], (SUBLANES, LANES))

            def unit_taps(src_ref, src_base, base, c=c, first=first, j0=j0, j1=j1, w_rows=w_rows, bias=bias):
                out = [y_ref.at[c, pl.ds(base + k, SUBLANES, stride=CONV_STRIDE), :] for k in range(CONV_STRIDE)]
                acc = [[bias if first else out[k][...], None] for k in range(CONV_STRIDE)]
                for jk in range(j0, j1 + CONV_STRIDE - 1):
                    v = src_ref[c, pl.ds(src_base + shift0 + jk, SUBLANES, stride=CONV_STRIDE), :]
                    for k in range(CONV_STRIDE):
                        j = jk - k
                        if j0 <= j < j1:
                            term = v * w_rows[j]
                            acc[k][j % 2] = term if acc[k][j % 2] is None else acc[k][j % 2] + term
                for k in range(CONV_STRIDE):
                    out[k][...] = acc[k][0] + acc[k][1]

            unit_taps(head_ref, 0, 0)

            def unit_body(r, carry, unit_taps=unit_taps):
                unit_taps(uc_ref, r * unit - CONV_HALO, r * unit)
                return carry

            lax.fori_loop(1, tc // unit, unit_body, 0, unroll=4)

    def activations(r0):
        y = jnp.concatenate([y_ref[c, r0:r0 + CONV_TAIL_ROWS, :] for c in range(n_slabs)], axis=1)
        return jax.nn.silu(_layer_norm(y, lng_ref[...], lnb_ref[...])).astype(BF16)

    act = activations(0)
    for r0 in range(0, tc, CONV_TAIL_ROWS):
        out = jnp.dot(act, wpw_ref[...], preferred_element_type=F32) + bpw_ref[...]
        if r0 + CONV_TAIL_ROWS < tc:
            act = activations(r0 + CONV_TAIL_ROWS)
        o_ref[r0:r0 + CONV_TAIL_ROWS, :] = out.astype(BF16)


def _conv_branch(u, wdw, bdw, lng, lnb, wpw, bpw, *, batch, seq, tc):
    n_slabs, t, _ = u.shape
    width = n_slabs * LANES
    tiles = seq // tc
    cur = pl.BlockSpec((n_slabs, tc, LANES), lambda b, i: (0, b * tiles + i, 0))
    prev = pl.BlockSpec((n_slabs, CONV_HALO, LANES),
                        lambda b, i: (0, b * (seq // CONV_HALO) + jnp.maximum(i * (tc // CONV_HALO) - 1, 0), 0))
    return pl.pallas_call(
        functools.partial(_conv_kernel, tc=tc),
        grid=(batch, tiles),
        in_specs=[cur, prev, _resident(wdw.shape), _resident(bdw.shape), _resident(lng.shape), _resident(lnb.shape),
                  _resident(wpw.shape), _resident(bpw.shape)],
        out_specs=pl.BlockSpec((tc, width), lambda b, i: (b * tiles + i, 0)),
        out_shape=jax.ShapeDtypeStruct((t, width), BF16),
        scratch_shapes=[pltpu.VMEM((n_slabs, 2 * CONV_HALO, LANES), F32), pltpu.VMEM((n_slabs, tc, LANES), F32)],
        compiler_params=pltpu.CompilerParams(dimension_semantics=("arbitrary", "arbitrary"),
                                             vmem_limit_bytes=VMEM_LIMIT_BYTES),
        name="conformer_conv",
    )(u, u, wdw, bdw, lng, lnb, wpw, bpw)


def _out_proj_kernel(a_ref, c_ref, x_ref, w_ref, b_ref, g_ref, beta_ref, o_ref, *, alpha, splits):
    attn_w = a_ref.shape[1]
    rows = a_ref.shape[0] // splits

    def matmuls(r0):
        mix = jnp.dot(a_ref[r0:r0 + rows, :], w_ref[:attn_w, :], preferred_element_type=F32)
        return mix + jnp.dot(c_ref[r0:r0 + rows, :], w_ref[attn_w:, :], preferred_element_type=F32)

    def norm(r0, mix):
        y = alpha * x_ref[r0:r0 + rows, :] + (mix + b_ref[...])
        o_ref[r0:r0 + rows, :] = _layer_norm(y, g_ref[...], beta_ref[...])

    mix = matmuls(0)
    for k in range(splits):
        nxt = matmuls((k + 1) * rows) if k + 1 < splits else None
        norm(k * rows, mix)
        mix = nxt


def _out_proj(attn, conv, x2, w, b, g, beta, *, alpha, tm, splits):
    t, d = x2.shape
    row = lambda width: pl.BlockSpec((tm, width), lambda i: (i, 0))
    return pl.pallas_call(
        functools.partial(_out_proj_kernel, alpha=alpha, splits=splits),
        grid=(t // tm,),
        in_specs=[row(attn.shape[1]), row(conv.shape[1]), row(d), _resident(w.shape), _resident(b.shape),
                  _resident(g.shape), _resident(beta.shape)],
        out_specs=row(d),
        out_shape=jax.ShapeDtypeStruct((t, d), F32),
        compiler_params=pltpu.CompilerParams(dimension_semantics=("arbitrary",), vmem_limit_bytes=VMEM_LIMIT_BYTES),
        name="out_proj_ln",
    )(attn, conv, x2, w, b, g, beta)


def _ffn_kernel(x_ref, wg_ref, wu_ref, wd_ref, g_ref, beta_ref, o_ref, xb_ref, *, alpha):
    f = pl.program_id(1)

    @pl.when(f == 0)
    def _():
        xb_ref[...] = x_ref[...].astype(BF16)

    @pl.when(f == 0)
    def _():
        o_ref[...] = alpha * x_ref[...]

    xb = xb_ref[...]
    tf = wg_ref.shape[1]
    hidden = []
    for c0 in range(0, tf, tf // 2):
        gate = jnp.dot(xb, wg_ref[:, c0:c0 + tf // 2], preferred_element_type=F32)
        up = jnp.dot(xb, wu_ref[:, c0:c0 + tf // 2], preferred_element_type=F32)
        hidden.append((jax.nn.silu(gate) * up).astype(BF16))
    half = o_ref.shape[1] // 2
    for h, c0 in zip(hidden, range(0, tf, tf // 2)):
        for n0 in (0, half):
            o_ref[:, n0:n0 + half] += jnp.dot(h, wd_ref[c0:c0 + tf // 2, n0:n0 + half], preferred_element_type=F32)

    @pl.when(f == pl.num_programs(1) - 1)
    def _():
        o_ref[...] = _layer_norm(o_ref[...], g_ref[...], beta_ref[...])


def _ffn(x1, wg, wu, wd, g, beta, *, alpha, tm, tf):
    t, d = x1.shape
    nf = wg.shape[1] // tf
    return pl.pallas_call(
        functools.partial(_ffn_kernel, alpha=alpha),
        grid=(t // tm, nf),
        in_specs=[pl.BlockSpec((tm, d), lambda i, f: (i, 0)),
                  pl.BlockSpec((d, tf), lambda i, f: (0, f)),
                  pl.BlockSpec((d, tf), lambda i, f: (0, f)),
                  pl.BlockSpec((tf, d), lambda i, f: (f, 0)),
                  _resident(g.shape), _resident(beta.shape)],
        out_specs=pl.BlockSpec((tm, d), lambda i, f: (i, 0)),
        out_shape=jax.ShapeDtypeStruct((t, d), F32),
        scratch_shapes=[pltpu.VMEM((tm, d), BF16)],
        compiler_params=pltpu.CompilerParams(dimension_semantics=("arbitrary", "arbitrary"),
                                             vmem_limit_bytes=VMEM_LIMIT_BYTES),
        name="swiglu_ffn_ln",
    )(x1, wg, wu, wd, g, beta)


def kernel(x, positions, w_in, b_in, sinks, w_dw, b_dw, conv_ln_g, conv_ln_b, w_pw2, b_pw2, w_out, b_out, ln1_g,
           ln1_b, w_gate, w_up, w_down, ln2_g, ln2_b):
    batch, seq, d = x.shape
    depth = w_in.shape[0]
    alpha = (2 * depth) ** 0.25
    t = batch * seq
    row = lambda a: a.reshape(1, -1).astype(F32)

    half = HEAD_DIM // 2
    inv_freq = 1.0 / (ROPE_THETA ** (jnp.arange(half, dtype=F32) * 2.0 / HEAD_DIM))
    invf_col = inv_freq.reshape(half, 1)
    pos_row = positions.reshape(1, t)
    n_qkv = (N_Q_HEADS + 2 * N_KV_HEADS) * HEAD_DIM

    x2 = x.reshape(t, d)
    for l in range(depth):
        qt, k, vt, u = _in_proj(x2, pos_row, invf_col, w_in[l][:, :n_qkv].T.astype(BF16),
                                b_in[l][:n_qkv].reshape(-1, 1).astype(F32), w_in[l][:, n_qkv:].astype(BF16),
                                row(b_in[l][n_qkv:]), tm=TOKEN_TILE)
        later_weights = [w.astype(F32) for w in (w_pw2[l], w_out[l], w_gate[l], w_up[l], w_down[l])]
        attn, (wpw, wout, wg, wu, wd) = _attention(sinks[l].astype(F32), qt, k, vt, later_weights, batch=batch,
                                                   seq=seq, tq=ATTN_TILE)
        wdw = jnp.repeat(w_dw[l].reshape(CONV_KERNEL, -1).astype(F32), SUBLANES, axis=0)
        conv = _conv_branch(u, wdw, row(b_dw[l]), row(conv_ln_g[l]), row(conv_ln_b[l]), wpw, row(b_pw2[l]),
                            batch=batch, seq=seq, tc=CONV_TILE)
        x2 = _out_proj(attn, conv, x2, wout, row(b_out[l]), row(ln1_g[l]), row(ln1_b[l]), alpha=alpha,
                       tm=TOKEN_TILE, splits=4)
        x2 = _ffn(x2, wg, wu, wd, row(ln2_g[l]), row(ln2_b[l]), alpha=alpha, tm=TOKEN_TILE, tf=FFN_CHUNK)
    return x2.reshape(batch, seq, d)
```

```python
import functools

import jax
import jax.numpy as jnp
from jax import lax
from jax.experimental import pallas as pl
from jax.experimental.pallas import tpu as pltpu

HEAD_DIM = 64
N_Q_HEADS = 16
N_KV_HEADS = 2
WINDOW = 128
BLOCK = 128
CONV_KERNEL = 31
ROPE_THETA = 10000.0
LN_EPS = 1e-5

LANES = 128
SUBLANES = 8
CONV_HALO = 32
CONV_STRIDE = 4
VMEM_LIMIT_BYTES = 60 * 1024 * 1024

TOKEN_TILE = 1024
ATTN_TILE = 1024
CONV_TILE = 1024
CONV_TAIL_ROWS = 256
FFN_CHUNK = 512

F32 = jnp.float32
BF16 = jnp.bfloat16


def _resident(shape):
    return pl.BlockSpec(shape, lambda *_: (0,) * len(shape), pipeline_mode=pl.Buffered(1))


def _layer_norm(y, g, b):
    mu = jnp.mean(y, axis=-1, keepdims=True)
    d = y - mu
    var = jnp.mean(d * d, axis=-1, keepdims=True)
    return d * lax.rsqrt(var + LN_EPS) * g + b


def _in_proj_kernel(x_ref, pos_ref, invf_ref, wt_ref, bt_ref, w_ref, b_ref, qt_ref, k_ref, vt_ref, u_ref, *,
                    attn_w, kv_w):
    xb = x_ref[...].astype(BF16)
    half = HEAD_DIM // 2
    ang = invf_ref[...] * pos_ref[...].astype(F32)
    cos = jnp.cos(ang)
    sin = jnp.sin(ang)

    def rope_t(h):
        x1, x2 = h[:half], h[half:]
        return jnp.concatenate([x1 * cos - x2 * sin, x2 * cos + x1 * sin], axis=0)

    ht = lax.dot_general(wt_ref[...], xb, (((1,), (1,)), ((), ())), preferred_element_type=F32) + bt_ref[...]
    for r0 in range(0, attn_w, HEAD_DIM):
        qt_ref[r0:r0 + HEAD_DIM, :] = (rope_t(ht[r0:r0 + HEAD_DIM]) * (HEAD_DIM ** -0.5)).astype(BF16)
    kt = jnp.concatenate([rope_t(ht[attn_w + r0:attn_w + r0 + HEAD_DIM]) for r0 in range(0, kv_w, HEAD_DIM)], axis=0)
    k_ref[...] = kt.T.astype(BF16)
    vt_ref[...] = ht[attn_w + kv_w:].astype(BF16)

    chunk = 4 * LANES
    conv_w = u_ref.shape[0] * LANES
    for c0 in range(0, conv_w, chunk):
        a = jnp.dot(xb, w_ref[:, c0:c0 + chunk], preferred_element_type=F32) + b_ref[:, c0:c0 + chunk]
        g = jnp.dot(xb, w_ref[:, conv_w + c0:conv_w + c0 + chunk], preferred_element_type=F32)
        g = g + b_ref[:, conv_w + c0:conv_w + c0 + chunk]
        u = a * jax.nn.sigmoid(g)
        for j in range(chunk // LANES):
            u_ref[c0 // LANES + j] = u[:, j * LANES:(j + 1) * LANES]


def _in_proj(x2, pos_row, invf_col, wt, bt, w, b, *, tm):
    t, d = x2.shape
    attn_w = N_Q_HEADS * HEAD_DIM
    kv_w = N_KV_HEADS * HEAD_DIM
    conv_w = w.shape[1] // 2
    assert wt.shape[0] == attn_w + 2 * kv_w and kv_w == LANES and conv_w % (4 * LANES) == 0
    row = lambda width: pl.BlockSpec((tm, width), lambda i: (i, 0))
    col = lambda height: pl.BlockSpec((height, tm), lambda i: (0, i))
    return pl.pallas_call(
        functools.partial(_in_proj_kernel, attn_w=attn_w, kv_w=kv_w),
        grid=(t // tm,),
        in_specs=[row(d), col(1), _resident(invf_col.shape), _resident(wt.shape), _resident(bt.shape),
                  _resident(w.shape), _resident(b.shape)],
        out_specs=[col(attn_w), row(kv_w), col(kv_w), pl.BlockSpec((conv_w // LANES, tm, LANES), lambda i: (0, i, 0))],
        out_shape=[jax.ShapeDtypeStruct((attn_w, t), BF16), jax.ShapeDtypeStruct((t, kv_w), BF16),
                   jax.ShapeDtypeStruct((kv_w, t), BF16), jax.ShapeDtypeStruct((conv_w // LANES, t, LANES), F32)],
        compiler_params=pltpu.CompilerParams(dimension_semantics=("arbitrary",), vmem_limit_bytes=VMEM_LIMIT_BYTES),
        name="in_proj",
    )(x2, pos_row, invf_col, wt, bt, w, b)


def _attn_kernel(sinks_ref, qt_ref, kc_ref, kp_ref, vtc_ref, vtp_ref, *rest, tq, n_cast):
    cast_in, o_ref, cast_out = rest[:n_cast], rest[n_cast], rest[n_cast + 1:2 * n_cast + 1]
    kfull_ref, vtfull_ref = rest[2 * n_cast + 1:]
    i = pl.program_id(1)
    kfull_ref[:BLOCK, :] = kp_ref[...]
    kfull_ref[BLOCK:, :] = kc_ref[...]
    vtfull_ref[:, :BLOCK] = vtp_ref[...]
    vtfull_ref[:, BLOCK:] = vtc_ref[...]

    assert WINDOW == BLOCK
    own = (lax.broadcasted_iota(jnp.int32, (BLOCK, BLOCK), 0) <= lax.broadcasted_iota(jnp.int32, (BLOCK, BLOCK), 1))
    heads_per_kv = N_Q_HEADS // N_KV_HEADS
    zero = jnp.zeros((), BF16)
    no_q = jnp.zeros((HEAD_DIM, 2 * BLOCK), BF16)

    def block_scores(j):
        cols = slice(j * BLOCK, (j + 1) * BLOCK)
        kwin = kfull_ref[j * BLOCK:(j + 2) * BLOCK, :]

        def scores(pair):
            r0 = 2 * pair * HEAD_DIM
            q2 = jnp.concatenate([qt_ref[r0:r0 + HEAD_DIM, cols], qt_ref[r0 + HEAD_DIM:r0 + 2 * HEAD_DIM, cols]], axis=1)
            rhs = [q2, no_q] if 2 * pair < heads_per_kv else [no_q, q2]
            return jnp.dot(kwin, jnp.concatenate(rhs, axis=0), preferred_element_type=F32)

        return [scores(pair) for pair in range(N_Q_HEADS // 2)]

    s_next = block_scores(0)
    for j in range(tq // BLOCK):
        s_all = s_next
        if (j + 1) * BLOCK < tq:
            s_next = block_scores(j + 1)
        vwin = vtfull_ref[:, j * BLOCK:(j + 2) * BLOCK]

        def logits(h):
            s = s_all[h // 2][:, (h % 2) * BLOCK:(h % 2 + 1) * BLOCK]
            before = s[:BLOCK]
            if j == 0:
                before = jnp.where(i > 0, before, -jnp.inf)
            return jnp.where(own, s[BLOCK:], before)

        heads = range(N_Q_HEADS)
        col_max = [jnp.maximum(jnp.max(logits(h), axis=0, keepdims=True), sinks_ref[h]) for h in heads]
        expo = [jnp.exp(logits(h) - col_max[h]) for h in heads]
        inv = [1.0 / (jnp.sum(expo[h], axis=0, keepdims=True) + jnp.exp(sinks_ref[h] - col_max[h])) for h in heads]

        def probs_t(h):
            p = (expo[h] * inv[h]).astype(BF16)
            return jnp.concatenate([jnp.where(own, zero, p), jnp.where(own, p, zero)], axis=0)

        for pair in range(N_Q_HEADS // 2):
            g = 2 * pair // heads_per_kv
            pt = jnp.concatenate([probs_t(2 * pair), probs_t(2 * pair + 1)], axis=1)
            ot = jnp.dot(vwin[g * HEAD_DIM:(g + 1) * HEAD_DIM, :], pt, preferred_element_type=F32)
            o2 = jnp.concatenate([ot[:, :BLOCK], ot[:, BLOCK:]], axis=0)
            o_ref[j * BLOCK:(j + 1) * BLOCK, pair * LANES:(pair + 1) * LANES] = o2.T.astype(BF16)

    for src, dst in zip(cast_in, cast_out):
        dst[...] = src[...].astype(BF16)


def _attention(sinks, qt, k, vt, to_bf16, *, batch, seq, tq):
    attn_w, t = qt.shape
    kvw = k.shape[1]
    tiles = seq // tq
    steps = batch * tiles
    prev_block = lambda b, i: b * (seq // BLOCK) + jnp.maximum(i * (tq // BLOCK) - 1, 0)
    assert all(w.shape[0] % (steps * 2 * SUBLANES) == 0 for w in to_bf16)
    slab = lambda w: pl.BlockSpec((w.shape[0] // steps, w.shape[1]), lambda b, i: (b * tiles + i, 0))
    outs = pl.pallas_call(
        functools.partial(_attn_kernel, tq=tq, n_cast=len(to_bf16)),
        grid=(batch, tiles),
        in_specs=[pl.BlockSpec(memory_space=pltpu.SMEM),
                  pl.BlockSpec((attn_w, tq), lambda b, i: (0, b * tiles + i)),
                  pl.BlockSpec((tq, kvw), lambda b, i: (b * tiles + i, 0)),
                  pl.BlockSpec((BLOCK, kvw), lambda b, i: (prev_block(b, i), 0)),
                  pl.BlockSpec((kvw, tq), lambda b, i: (0, b * tiles + i)),
                  pl.BlockSpec((kvw, BLOCK), lambda b, i: (0, prev_block(b, i)))] + [slab(w) for w in to_bf16],
        out_specs=[pl.BlockSpec((tq, attn_w), lambda b, i: (b * tiles + i, 0))] + [slab(w) for w in to_bf16],
        out_shape=[jax.ShapeDtypeStruct((t, attn_w), BF16)] + [jax.ShapeDtypeStruct(w.shape, BF16) for w in to_bf16],
        scratch_shapes=[pltpu.VMEM((tq + BLOCK, kvw), BF16), pltpu.VMEM((kvw, tq + BLOCK), BF16)],
        compiler_params=pltpu.CompilerParams(dimension_semantics=("arbitrary", "arbitrary"),
                                             vmem_limit_bytes=VMEM_LIMIT_BYTES),
        name="swa_attention",
    )(sinks, qt, k, k, vt, vt, *to_bf16)
    return outs[0], outs[1:]


def _conv_kernel(uc_ref, up_ref, wdw_ref, bdw_ref, lng_ref, lnb_ref, wpw_ref, bpw_ref, o_ref, head_ref, y_ref, *, tc):
    i = pl.program_id(1)
    n_slabs = uc_ref.shape[0]
    unit = SUBLANES * CONV_STRIDE
    assert unit == CONV_HALO
    halo = up_ref[...]
    head_ref[:, :CONV_HALO, :] = jnp.where(i > 0, halo, jnp.zeros_like(halo))
    head_ref[:, CONV_HALO:, :] = uc_ref[:, :unit, :]

    shift0 = CONV_HALO - (CONV_KERNEL - 1)
    passes = ((0, (CONV_KERNEL + 1) // 2), ((CONV_KERNEL + 1) // 2, CONV_KERNEL))
    for c in range(n_slabs):
        for first, (j0, j1) in zip((True, False), passes):
            w_rows = {j: wdw_ref[SUBLANES * j:SUBLANES * (j + 1), c * LANES:(c + 1) * LANES] for j in range(j0, j1)}
            bias = jnp.broadcast_to(bdw_ref[:, c * LANES:(c + 1) * LANES], (SUBLANES, LANES))

            def unit_taps(src_ref, src_base, base, c=c, first=first, j0=j0, j1=j1, w_rows=w_rows, bias=bias):
                out = [y_ref.at[c, pl.ds(base + k, SUBLANES, stride=CONV_STRIDE), :] for k in range(CONV_STRIDE)]
                acc = [[bias if first else out[k][...], None] for k in range(CONV_STRIDE)]
                for jk in range(j0, j1 + CONV_STRIDE - 1):
                    v = src_ref[c, pl.ds(src_base + shift0 + jk, SUBLANES, stride=CONV_STRIDE), :]
                    for k in range(CONV_STRIDE):
                        j = jk - k
                        if j0 <= j < j1:
                            term = v * w_rows[j]
                            acc[k][j % 2] = term if acc[k][j % 2] is None else acc[k][j % 2] + term
                for k in range(CONV_STRIDE):
                    out[k][...] = acc[k][0] + acc[k][1]

            unit_taps(head_ref, 0, 0)

            def unit_body(r, carry, unit_taps=unit_taps):
                unit_taps(uc_ref, r * unit - CONV_HALO, r * unit)
                return carry

            lax.fori_loop(1, tc // unit, unit_body, 0, unroll=4)

    def activations(r0):
        y = jnp.concatenate([y_ref[c, r0:r0 + CONV_TAIL_ROWS, :] for c in range(n_slabs)], axis=1)
        return jax.nn.silu(_layer_norm(y, lng_ref[...], lnb_ref[...])).astype(BF16)

    act = activations(0)
    for r0 in range(0, tc, CONV_TAIL_ROWS):
        out = jnp.dot(act, wpw_ref[...], preferred_element_type=F32) + bpw_ref[...]
        if r0 + CONV_TAIL_ROWS < tc:
            act = activations(r0 + CONV_TAIL_ROWS)
        o_ref[r0:r0 + CONV_TAIL_ROWS, :] = out.astype(BF16)


def _conv_branch(u, wdw, bdw, lng, lnb, wpw, bpw, *, batch, seq, tc):
    n_slabs, t, _ = u.shape
    width = n_slabs * LANES
    tiles = seq // tc
    cur = pl.BlockSpec((n_slabs, tc, LANES), lambda b, i: (0, b * tiles + i, 0))
    prev = pl.BlockSpec((n_slabs, CONV_HALO, LANES),
                        lambda b, i: (0, b * (seq // CONV_HALO) + jnp.maximum(i * (tc // CONV_HALO) - 1, 0), 0))
    return pl.pallas_call(
        functools.partial(_conv_kernel, tc=tc),
        grid=(batch, tiles),
        in_specs=[cur, prev, _resident(wdw.shape), _resident(bdw.shape), _resident(lng.shape), _resident(lnb.shape),
                  _resident(wpw.shape), _resident(bpw.shape)],
        out_specs=pl.BlockSpec((tc, width), lambda b, i: (b * tiles + i, 0)),
        out_shape=jax.ShapeDtypeStruct((t, width), BF16),
        scratch_shapes=[pltpu.VMEM((n_slabs, 2 * CONV_HALO, LANES), F32), pltpu.VMEM((n_slabs, tc, LANES), F32)],
        compiler_params=pltpu.CompilerParams(dimension_semantics=("arbitrary", "arbitrary"),
                                             vmem_limit_bytes=VMEM_LIMIT_BYTES),
        name="conformer_conv",
    )(u, u, wdw, bdw, lng, lnb, wpw, bpw)


def _out_proj_kernel(a_ref, c_ref, x_ref, w_ref, b_ref, g_ref, beta_ref, o_ref, *, alpha, splits):
    attn_w = a_ref.shape[1]
    rows = a_ref.shape[0] // splits

    def matmuls(r0):
        mix = jnp.dot(a_ref[r0:r0 + rows, :], w_ref[:attn_w, :], preferred_element_type=F32)
        return mix + jnp.dot(c_ref[r0:r0 + rows, :], w_ref[attn_w:, :], preferred_element_type=F32)

    def norm(r0, mix):
        y = alpha * x_ref[r0:r0 + rows, :] + (mix + b_ref[...])
        o_ref[r0:r0 + rows, :] = _layer_norm(y, g_ref[...], beta_ref[...])

    mix = matmuls(0)
    for k in range(splits):
        nxt = matmuls((k + 1) * rows) if k + 1 < splits else None
        norm(k * rows, mix)
        mix = nxt


def _out_proj(attn, conv, x2, w, b, g, beta, *, alpha, tm, splits):
    t, d = x2.shape
    row = lambda width: pl.BlockSpec((tm, width), lambda i: (i, 0))
    return pl.pallas_call(
        functools.partial(_out_proj_kernel, alpha=alpha, splits=splits),
        grid=(t // tm,),
        in_specs=[row(attn.shape[1]), row(conv.shape[1]), row(d), _resident(w.shape), _resident(b.shape),
                  _resident(g.shape), _resident(beta.shape)],
        out_specs=row(d),
        out_shape=jax.ShapeDtypeStruct((t, d), F32),
        compiler_params=pltpu.CompilerParams(dimension_semantics=("arbitrary",), vmem_limit_bytes=VMEM_LIMIT_BYTES),
        name="out_proj_ln",
    )(attn, conv, x2, w, b, g, beta)


def _ffn_kernel(x_ref, wg_ref, wu_ref, wd_ref, g_ref, beta_ref, o_ref, xb_ref, *, alpha):
    f = pl.program_id(1)

    @pl.when(f == 0)
    def _():
        o_ref[...] = alpha * x_ref[...]

    xb = x_ref[...].astype(BF16)
    tf = wg_ref.shape[1]
    hidden = []
    for c0 in range(0, tf, tf // 2):
        gate = jnp.dot(xb, wg_ref[:, c0:c0 + tf // 2], preferred_element_type=F32)
        up = jnp.dot(xb, wu_ref[:, c0:c0 + tf // 2], preferred_element_type=F32)
        hidden.append((jax.nn.silu(gate) * up).astype(BF16))
    for h, c0 in zip(hidden, range(0, tf, tf // 2)):
        o_ref[...] += jnp.dot(h, wd_ref[c0:c0 + tf // 2, :], preferred_element_type=F32)

    @pl.when(f == pl.num_programs(1) - 1)
    def _():
        o_ref[...] = _layer_norm(o_ref[...], g_ref[...], beta_ref[...])


def _ffn(x1, wg, wu, wd, g, beta, *, alpha, tm, tf):
    t, d = x1.shape
    nf = wg.shape[1] // tf
    return pl.pallas_call(
        functools.partial(_ffn_kernel, alpha=alpha),
        grid=(t // tm, nf),
        in_specs=[pl.BlockSpec((tm, d), lambda i, f: (i, 0)),
                  pl.BlockSpec((d, tf), lambda i, f: (0, f)),
                  pl.BlockSpec((d, tf), lambda i, f: (0, f)),
                  pl.BlockSpec((tf, d), lambda i, f: (f, 0)),
                  _resident(g.shape), _resident(beta.shape)],
        out_specs=pl.BlockSpec((tm, d), lambda i, f: (i, 0)),
        out_shape=jax.ShapeDtypeStruct((t, d), F32),
        scratch_shapes=[pltpu.VMEM((tm, d), BF16)],
        compiler_params=pltpu.CompilerParams(dimension_semantics=("arbitrary", "arbitrary"),
                                             vmem_limit_bytes=VMEM_LIMIT_BYTES),
        name="swiglu_ffn_ln",
    )(x1, wg, wu, wd, g, beta)


def kernel(x, positions, w_in, b_in, sinks, w_dw, b_dw, conv_ln_g, conv_ln_b, w_pw2, b_pw2, w_out, b_out, ln1_g,
           ln1_b, w_gate, w_up, w_down, ln2_g, ln2_b):
    batch, seq, d = x.shape
    depth = w_in.shape[0]
    alpha = (2 * depth) ** 0.25
    t = batch * seq
    row = lambda a: a.reshape(1, -1).astype(F32)

    half = HEAD_DIM // 2
    inv_freq = 1.0 / (ROPE_THETA ** (jnp.arange(half, dtype=F32) * 2.0 / HEAD_DIM))
    invf_col = inv_freq.reshape(half, 1)
    pos_row = positions.reshape(1, t)
    n_qkv = (N_Q_HEADS + 2 * N_KV_HEADS) * HEAD_DIM

    x2 = x.reshape(t, d)
    for l in range(depth):
        qt, k, vt, u = _in_proj(x2, pos_row, invf_col, w_in[l][:, :n_qkv].T.astype(BF16),
                                b_in[l][:n_qkv].reshape(-1, 1).astype(F32), w_in[l][:, n_qkv:].astype(BF16),
                                row(b_in[l][n_qkv:]), tm=TOKEN_TILE)
        later_weights = [w.astype(F32) for w in (w_pw2[l], w_out[l], w_gate[l], w_up[l], w_down[l])]
        attn, (wpw, wout, wg, wu, wd) = _attention(sinks[l].astype(F32), qt, k, vt, later_weights, batch=batch,
                                                   seq=seq, tq=ATTN_TILE)
        wdw = jnp.repeat(w_dw[l].reshape(CONV_KERNEL, -1).astype(F32), SUBLANES, axis=0)
        conv = _conv_branch(u, wdw, row(b_dw[l]), row(conv_ln_g[l]), row(conv_ln_b[l]), wpw, row(b_pw2[l]),
                            batch=batch, seq=seq, tc=CONV_TILE)
        x2 = _out_proj(attn, conv, x2, wout, row(b_out[l]), row(ln1_g[l]), row(ln1_b[l]), alpha=alpha,
                       tm=TOKEN_TILE, splits=4)
        x2 = _ffn(x2, wg, wu, wd, row(ln2_g[l]), row(ln2_b[l]), alpha=alpha, tm=TOKEN_TILE, tf=FFN_CHUNK)
    return x2.reshape(batch, seq, d)
```

```python
import functools

import jax
import jax.numpy as jnp
from jax import lax
from jax.experimental import pallas as pl
from jax.experimental.pallas import tpu as pltpu

HEAD_DIM = 64
N_Q_HEADS = 16
N_KV_HEADS = 2
WINDOW = 128
BLOCK = 128
CONV_KERNEL = 31
ROPE_THETA = 10000.0
LN_EPS = 1e-5

LANES = 128
SUBLANES = 8
CONV_HALO = 32
CONV_STRIDE = 4
VMEM_LIMIT_BYTES = 60 * 1024 * 1024

TOKEN_TILE = 1024
ATTN_TILE = 2048
CONV_TILE = 2048
CONV_TAIL_ROWS = 256
FFN_CHUNK = 512

F32 = jnp.float32
BF16 = jnp.bfloat16


def _resident(shape):
    return pl.BlockSpec(shape, lambda *_: (0,) * len(shape), pipeline_mode=pl.Buffered(1))


def _layer_norm(y, g, b):
    mu = jnp.mean(y, axis=-1, keepdims=True)
    d = y - mu
    var = jnp.mean(d * d, axis=-1, keepdims=True)
    return d * lax.rsqrt(var + LN_EPS) * g + b


def _in_proj_kernel(x_ref, pos_ref, invf_ref, wt_ref, bt_ref, w_ref, b_ref, qt_ref, k_ref, vt_ref, u_ref, *,
                    attn_w, kv_w):
    xb = x_ref[...].astype(BF16)
    half = HEAD_DIM // 2
    ang = invf_ref[...] * pos_ref[...].astype(F32)
    cos = jnp.cos(ang)
    sin = jnp.sin(ang)

    def rope_t(h):
        x1, x2 = h[:half], h[half:]
        return jnp.concatenate([x1 * cos - x2 * sin, x2 * cos + x1 * sin], axis=0)

    ht = lax.dot_general(wt_ref[...], xb, (((1,), (1,)), ((), ())), preferred_element_type=F32) + bt_ref[...]
    for r0 in range(0, attn_w, HEAD_DIM):
        qt_ref[r0:r0 + HEAD_DIM, :] = (rope_t(ht[r0:r0 + HEAD_DIM]) * (HEAD_DIM ** -0.5)).astype(BF16)
    kt = jnp.concatenate([rope_t(ht[attn_w + r0:attn_w + r0 + HEAD_DIM]) for r0 in range(0, kv_w, HEAD_DIM)], axis=0)
    k_ref[...] = kt.T.astype(BF16)
    vt_ref[...] = ht[attn_w + kv_w:].astype(BF16)

    chunk = 4 * LANES
    conv_w = u_ref.shape[0] * LANES
    for c0 in range(0, conv_w, chunk):
        a = jnp.dot(xb, w_ref[:, c0:c0 + chunk], preferred_element_type=F32) + b_ref[:, c0:c0 + chunk]
        g = jnp.dot(xb, w_ref[:, conv_w + c0:conv_w + c0 + chunk], preferred_element_type=F32)
        g = g + b_ref[:, conv_w + c0:conv_w + c0 + chunk]
        u = a * jax.nn.sigmoid(g)
        for j in range(chunk // LANES):
            u_ref[c0 // LANES + j] = u[:, j * LANES:(j + 1) * LANES]


def _in_proj(x2, pos_row, invf_col, wt, bt, w, b, *, tm):
    t, d = x2.shape
    attn_w = N_Q_HEADS * HEAD_DIM
    kv_w = N_KV_HEADS * HEAD_DIM
    conv_w = w.shape[1] // 2
    assert wt.shape[0] == attn_w + 2 * kv_w and kv_w == LANES and conv_w % (4 * LANES) == 0
    row = lambda width: pl.BlockSpec((tm, width), lambda i: (i, 0))
    col = lambda height: pl.BlockSpec((height, tm), lambda i: (0, i))
    return pl.pallas_call(
        functools.partial(_in_proj_kernel, attn_w=attn_w, kv_w=kv_w),
        grid=(t // tm,),
        in_specs=[row(d), col(1), _resident(invf_col.shape), _resident(wt.shape), _resident(bt.shape),
                  _resident(w.shape), _resident(b.shape)],
        out_specs=[col(attn_w), row(kv_w), col(kv_w), pl.BlockSpec((conv_w // LANES, tm, LANES), lambda i: (0, i, 0))],
        out_shape=[jax.ShapeDtypeStruct((attn_w, t), BF16), jax.ShapeDtypeStruct((t, kv_w), BF16),
                   jax.ShapeDtypeStruct((kv_w, t), BF16), jax.ShapeDtypeStruct((conv_w // LANES, t, LANES), F32)],
        compiler_params=pltpu.CompilerParams(dimension_semantics=("arbitrary",), vmem_limit_bytes=VMEM_LIMIT_BYTES),
        name="in_proj",
    )(x2, pos_row, invf_col, wt, bt, w, b)


def _attn_kernel(sinks_ref, qt_ref, kc_ref, kp_ref, vtc_ref, vtp_ref, *rest, tq, n_cast):
    cast_in, o_ref, cast_out = rest[:n_cast], rest[n_cast], rest[n_cast + 1:2 * n_cast + 1]
    kfull_ref, vtfull_ref = rest[2 * n_cast + 1:]
    i = pl.program_id(1)
    kfull_ref[:BLOCK, :] = kp_ref[...]
    kfull_ref[BLOCK:, :] = kc_ref[...]
    vtfull_ref[:, :BLOCK] = vtp_ref[...]
    vtfull_ref[:, BLOCK:] = vtc_ref[...]

    assert WINDOW == BLOCK
    own = (lax.broadcasted_iota(jnp.int32, (BLOCK, BLOCK), 0) <= lax.broadcasted_iota(jnp.int32, (BLOCK, BLOCK), 1))
    heads_per_kv = N_Q_HEADS // N_KV_HEADS
    zero = jnp.zeros((), BF16)
    no_q = jnp.zeros((HEAD_DIM, 2 * BLOCK), BF16)

    def block_scores(j):
        cols = slice(j * BLOCK, (j + 1) * BLOCK)
        kwin = kfull_ref[j * BLOCK:(j + 2) * BLOCK, :]

        def scores(pair):
            r0 = 2 * pair * HEAD_DIM
            q2 = jnp.concatenate([qt_ref[r0:r0 + HEAD_DIM, cols], qt_ref[r0 + HEAD_DIM:r0 + 2 * HEAD_DIM, cols]], axis=1)
            rhs = [q2, no_q] if 2 * pair < heads_per_kv else [no_q, q2]
            return jnp.dot(kwin, jnp.concatenate(rhs, axis=0), preferred_element_type=F32)

        return [scores(pair) for pair in range(N_Q_HEADS // 2)]

    s_next = block_scores(0)
    for j in range(tq // BLOCK):
        s_all = s_next
        if (j + 1) * BLOCK < tq:
            s_next = block_scores(j + 1)
        vwin = vtfull_ref[:, j * BLOCK:(j + 2) * BLOCK]

        def logits(h):
            s = s_all[h // 2][:, (h % 2) * BLOCK:(h % 2 + 1) * BLOCK]
            before = s[:BLOCK]
            if j == 0:
                before = jnp.where(i > 0, before, -jnp.inf)
            return jnp.where(own, s[BLOCK:], before)

        heads = range(N_Q_HEADS)
        col_max = [jnp.maximum(jnp.max(logits(h), axis=0, keepdims=True), sinks_ref[h]) for h in heads]
        expo = [jnp.exp(logits(h) - col_max[h]) for h in heads]
        inv = [1.0 / (jnp.sum(expo[h], axis=0, keepdims=True) + jnp.exp(sinks_ref[h] - col_max[h])) for h in heads]

        def probs_t(h):
            p = (expo[h] * inv[h]).astype(BF16)
            return jnp.concatenate([jnp.where(own, zero, p), jnp.where(own, p, zero)], axis=0)

        for pair in range(N_Q_HEADS // 2):
            g = 2 * pair // heads_per_kv
            pt = jnp.concatenate([probs_t(2 * pair), probs_t(2 * pair + 1)], axis=1)
            ot = jnp.dot(vwin[g * HEAD_DIM:(g + 1) * HEAD_DIM, :], pt, preferred_element_type=F32)
            o2 = jnp.concatenate([ot[:, :BLOCK], ot[:, BLOCK:]], axis=0)
            o_ref[j * BLOCK:(j + 1) * BLOCK, pair * LANES:(pair + 1) * LANES] = o2.T.astype(BF16)

    for src, dst in zip(cast_in, cast_out):
        dst[...] = src[...].astype(BF16)


def _attention(sinks, qt, k, vt, to_bf16, *, batch, seq, tq):
    attn_w, t = qt.shape
    kvw = k.shape[1]
    tiles = seq // tq
    steps = batch * tiles
    prev_block = lambda b, i: b * (seq // BLOCK) + jnp.maximum(i * (tq // BLOCK) - 1, 0)
    assert all(w.shape[0] % (steps * 2 * SUBLANES) == 0 for w in to_bf16)
    slab = lambda w: pl.BlockSpec((w.shape[0] // steps, w.shape[1]), lambda b, i: (b * tiles + i, 0))
    outs = pl.pallas_call(
        functools.partial(_attn_kernel, tq=tq, n_cast=len(to_bf16)),
        grid=(batch, tiles),
        in_specs=[pl.BlockSpec(memory_space=pltpu.SMEM),
                  pl.BlockSpec((attn_w, tq), lambda b, i: (0, b * tiles + i)),
                  pl.BlockSpec((tq, kvw), lambda b, i: (b * tiles + i, 0)),
                  pl.BlockSpec((BLOCK, kvw), lambda b, i: (prev_block(b, i), 0)),
                  pl.BlockSpec((kvw, tq), lambda b, i: (0, b * tiles + i)),
                  pl.BlockSpec((kvw, BLOCK), lambda b, i: (0, prev_block(b, i)))] + [slab(w) for w in to_bf16],
        out_specs=[pl.BlockSpec((tq, attn_w), lambda b, i: (b * tiles + i, 0))] + [slab(w) for w in to_bf16],
        out_shape=[jax.ShapeDtypeStruct((t, attn_w), BF16)] + [jax.ShapeDtypeStruct(w.shape, BF16) for w in to_bf16],
        scratch_shapes=[pltpu.VMEM((tq + BLOCK, kvw), BF16), pltpu.VMEM((kvw, tq + BLOCK), BF16)],
        compiler_params=pltpu.CompilerParams(dimension_semantics=("arbitrary", "arbitrary"),
                                             vmem_limit_bytes=VMEM_LIMIT_BYTES),
        name="swa_attention",
    )(sinks, qt, k, k, vt, vt, *to_bf16)
    return outs[0], outs[1:]


def _conv_kernel(uc_ref, up_ref, wdw_ref, bdw_ref, lng_ref, lnb_ref, wpw_ref, bpw_ref, o_ref, head_ref, y_ref, *, tc):
    i = pl.program_id(1)
    n_slabs = uc_ref.shape[0]
    unit = SUBLANES * CONV_STRIDE
    assert unit == CONV_HALO
    halo = up_ref[...]
    head_ref[:, :CONV_HALO, :] = jnp.where(i > 0, halo, jnp.zeros_like(halo))
    head_ref[:, CONV_HALO:, :] = uc_ref[:, :unit, :]

    shift0 = CONV_HALO - (CONV_KERNEL - 1)
    passes = ((0, (CONV_KERNEL + 1) // 2), ((CONV_KERNEL + 1) // 2, CONV_KERNEL))
    for c in range(n_slabs):
        for first, (j0, j1) in zip((True, False), passes):
            w_rows = {j: wdw_ref[SUBLANES * j:SUBLANES * (j + 1), c * LANES:(c + 1) * LANES] for j in range(j0, j1)}
            bias = jnp.broadcast_to(bdw_ref[:, c * LANES:(c + 1) * LANES], (SUBLANES, LANES))

            def unit_taps(src_ref, src_base, base, c=c, first=first, j0=j0, j1=j1, w_rows=w_rows, bias=bias):
                out = [y_ref.at[c, pl.ds(base + k, SUBLANES, stride=CONV_STRIDE), :] for k in range(CONV_STRIDE)]
                acc = [[bias if first else out[k][...], None] for k in range(CONV_STRIDE)]
                for jk in range(j0, j1 + CONV_STRIDE - 1):
                    v = src_ref[c, pl.ds(src_base + shift0 + jk, SUBLANES, stride=CONV_STRIDE), :]
                    for k in range(CONV_STRIDE):
                        j = jk - k
                        if j0 <= j < j1:
                            term = v * w_rows[j]
                            acc[k][j % 2] = term if acc[k][j % 2] is None else acc[k][j % 2] + term
                for k in range(CONV_STRIDE):
                    out[k][...] = acc[k][0] + acc[k][1]

            unit_taps(head_ref, 0, 0)

            def unit_body(r, carry, unit_taps=unit_taps):
                unit_taps(uc_ref, r * unit - CONV_HALO, r * unit)
                return carry

            lax.fori_loop(1, tc // unit, unit_body, 0, unroll=4)

    def activations(r0):
        y = jnp.concatenate([y_ref[c, r0:r0 + CONV_TAIL_ROWS, :] for c in range(n_slabs)], axis=1)
        return jax.nn.silu(_layer_norm(y, lng_ref[...], lnb_ref[...])).astype(BF16)

    act = activations(0)
    for r0 in range(0, tc, CONV_TAIL_ROWS):
        out = jnp.dot(act, wpw_ref[...], preferred_element_type=F32) + bpw_ref[...]
        if r0 + CONV_TAIL_ROWS < tc:
            act = activations(r0 + CONV_TAIL_ROWS)
        o_ref[r0:r0 + CONV_TAIL_ROWS, :] = out.astype(BF16)


def _conv_branch(u, wdw, bdw, lng, lnb, wpw, bpw, *, batch, seq, tc):
    n_slabs, t, _ = u.shape
    width = n_slabs * LANES
    tiles = seq // tc
    cur = pl.BlockSpec((n_slabs, tc, LANES), lambda b, i: (0, b * tiles + i, 0))
    prev = pl.BlockSpec((n_slabs, CONV_HALO, LANES),
                        lambda b, i: (0, b * (seq // CONV_HALO) + jnp.maximum(i * (tc // CONV_HALO) - 1, 0), 0))
    return pl.pallas_call(
        functools.partial(_conv_kernel, tc=tc),
        grid=(batch, tiles),
        in_specs=[cur, prev, _resident(wdw.shape), _resident(bdw.shape), _resident(lng.shape), _resident(lnb.shape),
                  _resident(wpw.shape), _resident(bpw.shape)],
        out_specs=pl.BlockSpec((tc, width), lambda b, i: (b * tiles + i, 0)),
        out_shape=jax.ShapeDtypeStruct((t, width), BF16),
        scratch_shapes=[pltpu.VMEM((n_slabs, 2 * CONV_HALO, LANES), F32), pltpu.VMEM((n_slabs, tc, LANES), F32)],
        compiler_params=pltpu.CompilerParams(dimension_semantics=("arbitrary", "arbitrary"),
                                             vmem_limit_bytes=VMEM_LIMIT_BYTES),
        name="conformer_conv",
    )(u, u, wdw, bdw, lng, lnb, wpw, bpw)


def _out_proj_kernel(a_ref, c_ref, x_ref, w_ref, b_ref, g_ref, beta_ref, o_ref, *, alpha, splits):
    attn_w = a_ref.shape[1]
    rows = a_ref.shape[0] // splits

    def matmuls(r0):
        mix = jnp.dot(a_ref[r0:r0 + rows, :], w_ref[:attn_w, :], preferred_element_type=F32)
        return mix + jnp.dot(c_ref[r0:r0 + rows, :], w_ref[attn_w:, :], preferred_element_type=F32)

    def norm(r0, mix):
        y = alpha * x_ref[r0:r0 + rows, :] + (mix + b_ref[...])
        o_ref[r0:r0 + rows, :] = _layer_norm(y, g_ref[...], beta_ref[...])

    mix = matmuls(0)
    for k in range(splits):
        nxt = matmuls((k + 1) * rows) if k + 1 < splits else None
        norm(k * rows, mix)
        mix = nxt


def _out_proj(attn, conv, x2, w, b, g, beta, *, alpha, tm, splits):
    t, d = x2.shape
    row = lambda width: pl.BlockSpec((tm, width), lambda i: (i, 0))
    return pl.pallas_call(
        functools.partial(_out_proj_kernel, alpha=alpha, splits=splits),
        grid=(t // tm,),
        in_specs=[row(attn.shape[1]), row(conv.shape[1]), row(d), _resident(w.shape), _resident(b.shape),
                  _resident(g.shape), _resident(beta.shape)],
        out_specs=row(d),
        out_shape=jax.ShapeDtypeStruct((t, d), F32),
        compiler_params=pltpu.CompilerParams(dimension_semantics=("arbitrary",), vmem_limit_bytes=VMEM_LIMIT_BYTES),
        name="out_proj_ln",
    )(attn, conv, x2, w, b, g, beta)


def _ffn_kernel(x_ref, wg_ref, wu_ref, wd_ref, g_ref, beta_ref, o_ref, *, alpha):
    f = pl.program_id(1)

    @pl.when(f == 0)
    def _():
        o_ref[...] = alpha * x_ref[...]

    xb = x_ref[...].astype(BF16)
    tf = wg_ref.shape[1]
    hidden = []
    for c0 in range(0, tf, tf // 2):
        gate = jnp.dot(xb, wg_ref[:, c0:c0 + tf // 2], preferred_element_type=F32)
        up = jnp.dot(xb, wu_ref[:, c0:c0 + tf // 2], preferred_element_type=F32)
        hidden.append((jax.nn.silu(gate) * up).astype(BF16))
    for h, c0 in zip(hidden, range(0, tf, tf // 2)):
        o_ref[...] += jnp.dot(h, wd_ref[c0:c0 + tf // 2, :], preferred_element_type=F32)

    @pl.when(f == pl.num_programs(1) - 1)
    def _():
        o_ref[...] = _layer_norm(o_ref[...], g_ref[...], beta_ref[...])


def _ffn(x1, wg, wu, wd, g, beta, *, alpha, tm, tf):
    t, d = x1.shape
    nf = wg.shape[1] // tf
    return pl.pallas_call(
        functools.partial(_ffn_kernel, alpha=alpha),
        grid=(t // tm, nf),
        in_specs=[pl.BlockSpec((tm, d), lambda i, f: (i, 0)),
                  pl.BlockSpec((d, tf), lambda i, f: (0, f)),
                  pl.BlockSpec((d, tf), lambda i, f: (0, f)),
                  pl.BlockSpec((tf, d), lambda i, f: (f, 0)),
                  _resident(g.shape), _resident(beta.shape)],
        out_specs=pl.BlockSpec((tm, d), lambda i, f: (i, 0)),
        out_shape=jax.ShapeDtypeStruct((t, d), F32),
        compiler_params=pltpu.CompilerParams(dimension_semantics=("arbitrary", "arbitrary"),
                                             vmem_limit_bytes=VMEM_LIMIT_BYTES),
        name="swiglu_ffn_ln",
    )(x1, wg, wu, wd, g, beta)


def kernel(x, positions, w_in, b_in, sinks, w_dw, b_dw, conv_ln_g, conv_ln_b, w_pw2, b_pw2, w_out, b_out, ln1_g,
           ln1_b, w_gate, w_up, w_down, ln2_g, ln2_b):
    batch, seq, d = x.shape
    depth = w_in.shape[0]
    alpha = (2 * depth) ** 0.25
    t = batch * seq
    row = lambda a: a.reshape(1, -1).astype(F32)

    half = HEAD_DIM // 2
    inv_freq = 1.0 / (ROPE_THETA ** (jnp.arange(half, dtype=F32) * 2.0 / HEAD_DIM))
    invf_col = inv_freq.reshape(half, 1)
    pos_row = positions.reshape(1, t)
    n_qkv = (N_Q_HEADS + 2 * N_KV_HEADS) * HEAD_DIM

    x2 = x.reshape(t, d)
    for l in range(depth):
        qt, k, vt, u = _in_proj(x2, pos_row, invf_col, w_in[l][:, :n_qkv].T.astype(BF16),
                                b_in[l][:n_qkv].reshape(-1, 1).astype(F32), w_in[l][:, n_qkv:].astype(BF16),
                                row(b_in[l][n_qkv:]), tm=TOKEN_TILE)
        later_weights = [w.astype(F32) for w in (w_pw2[l], w_out[l], w_gate[l], w_up[l], w_down[l])]
        attn, (wpw, wout, wg, wu, wd) = _attention(sinks[l].astype(F32), qt, k, vt, later_weights, batch=batch,
                                                   seq=seq, tq=ATTN_TILE)
        wdw = jnp.repeat(w_dw[l].reshape(CONV_KERNEL, -1).astype(F32), SUBLANES, axis=0)
        conv = _conv_branch(u, wdw, row(b_dw[l]), row(conv_ln_g[l]), row(conv_ln_b[l]), wpw, row(b_pw2[l]),
                            batch=batch, seq=seq, tc=CONV_TILE)
        x2 = _out_proj(attn, conv, x2, wout, row(b_out[l]), row(ln1_g[l]), row(ln1_b[l]), alpha=alpha,
                       tm=TOKEN_TILE, splits=4)
        x2 = _ffn(x2, wg, wu, wd, row(ln2_g[l]), row(ln2_b[l]), alpha=alpha, tm=TOKEN_TILE, tf=FFN_CHUNK)
    return x2.reshape(batch, seq, d)
```

```python
import functools

import jax
import jax.numpy as jnp
from jax import lax
from jax.experimental import pallas as pl
from jax.experimental.pallas import tpu as pltpu

HEAD_DIM = 64
N_Q_HEADS = 16
N_KV_HEADS = 2
WINDOW = 128
BLOCK = 128
CONV_KERNEL = 31
ROPE_THETA = 10000.0
LN_EPS = 1e-5

LANES = 128
SUBLANES = 8
CONV_HALO = 32
CONV_STRIDE = 4
VMEM_LIMIT_BYTES = 60 * 1024 * 1024

TOKEN_TILE = 1024
ATTN_TILE = 2048
CONV_TILE = 2048
CONV_TAIL_ROWS = 256
FFN_CHUNK = 512

F32 = jnp.float32
BF16 = jnp.bfloat16


def _resident(shape):
    return pl.BlockSpec(shape, lambda *_: (0,) * len(shape), pipeline_mode=pl.Buffered(1))


def _layer_norm(y, g, b):
    mu = jnp.mean(y, axis=-1, keepdims=True)
    d = y - mu
    var = jnp.mean(d * d, axis=-1, keepdims=True)
    return d * lax.rsqrt(var + LN_EPS) * g + b


def _in_proj_kernel(x_ref, pos_ref, invf_ref, wt_ref, bt_ref, w_ref, b_ref, qt_ref, k_ref, vt_ref, u_ref, *,
                    attn_w, kv_w):
    xb = x_ref[...].astype(BF16)
    half = HEAD_DIM // 2
    ang = invf_ref[...] * pos_ref[...].astype(F32)
    cos = jnp.cos(ang)
    sin = jnp.sin(ang)

    def rope_t(h):
        x1, x2 = h[:half], h[half:]
        return jnp.concatenate([x1 * cos - x2 * sin, x2 * cos + x1 * sin], axis=0)

    ht = lax.dot_general(wt_ref[...], xb, (((1,), (1,)), ((), ())), preferred_element_type=F32) + bt_ref[...]
    for r0 in range(0, attn_w, HEAD_DIM):
        qt_ref[r0:r0 + HEAD_DIM, :] = (rope_t(ht[r0:r0 + HEAD_DIM]) * (HEAD_DIM ** -0.5)).astype(BF16)
    kt = jnp.concatenate([rope_t(ht[attn_w + r0:attn_w + r0 + HEAD_DIM]) for r0 in range(0, kv_w, HEAD_DIM)], axis=0)
    k_ref[...] = kt.T.astype(BF16)
    vt_ref[...] = ht[attn_w + kv_w:].astype(BF16)

    chunk = 4 * LANES
    conv_w = u_ref.shape[0] * LANES
    for c0 in range(0, conv_w, chunk):
        a = jnp.dot(xb, w_ref[:, c0:c0 + chunk], preferred_element_type=F32) + b_ref[:, c0:c0 + chunk]
        g = jnp.dot(xb, w_ref[:, conv_w + c0:conv_w + c0 + chunk], preferred_element_type=F32)
        g = g + b_ref[:, conv_w + c0:conv_w + c0 + chunk]
        u = a * jax.nn.sigmoid(g)
        for j in range(chunk // LANES):
            u_ref[c0 // LANES + j] = u[:, j * LANES:(j + 1) * LANES]


def _in_proj(x2, pos_row, invf_col, wt, bt, w, b, *, tm):
    t, d = x2.shape
    attn_w = N_Q_HEADS * HEAD_DIM
    kv_w = N_KV_HEADS * HEAD_DIM
    conv_w = w.shape[1] // 2
    assert wt.shape[0] == attn_w + 2 * kv_w and kv_w == LANES and conv_w % (4 * LANES) == 0
    row = lambda width: pl.BlockSpec((tm, width), lambda i: (i, 0))
    col = lambda height: pl.BlockSpec((height, tm), lambda i: (0, i))
    return pl.pallas_call(
        functools.partial(_in_proj_kernel, attn_w=attn_w, kv_w=kv_w),
        grid=(t // tm,),
        in_specs=[row(d), col(1), _resident(invf_col.shape), _resident(wt.shape), _resident(bt.shape),
                  _resident(w.shape), _resident(b.shape)],
        out_specs=[col(attn_w), row(kv_w), col(kv_w), pl.BlockSpec((conv_w // LANES, tm, LANES), lambda i: (0, i, 0))],
        out_shape=[jax.ShapeDtypeStruct((attn_w, t), BF16), jax.ShapeDtypeStruct((t, kv_w), BF16),
                   jax.ShapeDtypeStruct((kv_w, t), BF16), jax.ShapeDtypeStruct((conv_w // LANES, t, LANES), F32)],
        compiler_params=pltpu.CompilerParams(dimension_semantics=("arbitrary",), vmem_limit_bytes=VMEM_LIMIT_BYTES),
        name="in_proj",
    )(x2, pos_row, invf_col, wt, bt, w, b)


def _attn_kernel(sinks_ref, qt_ref, kc_ref, kp_ref, vtc_ref, vtp_ref, *rest, tq, n_cast):
    cast_in, o_ref, cast_out = rest[:n_cast], rest[n_cast], rest[n_cast + 1:2 * n_cast + 1]
    kfull_ref, vtfull_ref = rest[2 * n_cast + 1:]
    i = pl.program_id(1)
    kfull_ref[:BLOCK, :] = kp_ref[...]
    kfull_ref[BLOCK:, :] = kc_ref[...]
    vtfull_ref[:, :BLOCK] = vtp_ref[...]
    vtfull_ref[:, BLOCK:] = vtc_ref[...]

    assert WINDOW == BLOCK
    own = (lax.broadcasted_iota(jnp.int32, (BLOCK, BLOCK), 0) <= lax.broadcasted_iota(jnp.int32, (BLOCK, BLOCK), 1))
    heads_per_kv = N_Q_HEADS // N_KV_HEADS
    zero = jnp.zeros((), BF16)
    no_q = jnp.zeros((HEAD_DIM, 2 * BLOCK), BF16)

    def block_scores(j):
        cols = slice(j * BLOCK, (j + 1) * BLOCK)
        kwin = kfull_ref[j * BLOCK:(j + 2) * BLOCK, :]

        def scores(pair):
            r0 = 2 * pair * HEAD_DIM
            q2 = jnp.concatenate([qt_ref[r0:r0 + HEAD_DIM, cols], qt_ref[r0 + HEAD_DIM:r0 + 2 * HEAD_DIM, cols]], axis=1)
            rhs = [q2, no_q] if 2 * pair < heads_per_kv else [no_q, q2]
            return jnp.dot(kwin, jnp.concatenate(rhs, axis=0), preferred_element_type=F32)

        return [scores(pair) for pair in range(N_Q_HEADS // 2)]

    s_next = block_scores(0)
    for j in range(tq // BLOCK):
        s_all = s_next
        if (j + 1) * BLOCK < tq:
            s_next = block_scores(j + 1)
        vwin = vtfull_ref[:, j * BLOCK:(j + 2) * BLOCK]

        def logits(h):
            s = s_all[h // 2][:, (h % 2) * BLOCK:(h % 2 + 1) * BLOCK]
            before = s[:BLOCK]
            if j == 0:
                before = jnp.where(i > 0, before, -jnp.inf)
            return jnp.where(own, s[BLOCK:], before)

        heads = range(N_Q_HEADS)
        col_max = [jnp.maximum(jnp.max(logits(h), axis=0, keepdims=True), sinks_ref[h]) for h in heads]
        expo = [jnp.exp(logits(h) - col_max[h]) for h in heads]
        inv = [1.0 / (jnp.sum(expo[h], axis=0, keepdims=True) + jnp.exp(sinks_ref[h] - col_max[h])) for h in heads]

        def probs_t(h):
            p = (expo[h] * inv[h]).astype(BF16)
            return jnp.concatenate([jnp.where(own, zero, p), jnp.where(own, p, zero)], axis=0)

        for pair in range(N_Q_HEADS // 2):
            g = 2 * pair // heads_per_kv
            pt = jnp.concatenate([probs_t(2 * pair), probs_t(2 * pair + 1)], axis=1)
            ot = jnp.dot(vwin[g * HEAD_DIM:(g + 1) * HEAD_DIM, :], pt, preferred_element_type=F32)
            o2 = jnp.concatenate([ot[:, :BLOCK], ot[:, BLOCK:]], axis=0)
            o_ref[j * BLOCK:(j + 1) * BLOCK, pair * LANES:(pair + 1) * LANES] = o2.T.astype(BF16)

    for src, dst in zip(cast_in, cast_out):
        dst[...] = src[...].astype(BF16)


def _attention(sinks, qt, k, vt, to_bf16, *, batch, seq, tq):
    attn_w, t = qt.shape
    kvw = k.shape[1]
    tiles = seq // tq
    steps = batch * tiles
    prev_block = lambda b, i: b * (seq // BLOCK) + jnp.maximum(i * (tq // BLOCK) - 1, 0)
    assert all(w.shape[0] % (steps * 2 * SUBLANES) == 0 for w in to_bf16)
    slab = lambda w: pl.BlockSpec((w.shape[0] // steps, w.shape[1]), lambda b, i: (b * tiles + i, 0))
    outs = pl.pallas_call(
        functools.partial(_attn_kernel, tq=tq, n_cast=len(to_bf16)),
        grid=(batch, tiles),
        in_specs=[pl.BlockSpec(memory_space=pltpu.SMEM),
                  pl.BlockSpec((attn_w, tq), lambda b, i: (0, b * tiles + i)),
                  pl.BlockSpec((tq, kvw), lambda b, i: (b * tiles + i, 0)),
                  pl.BlockSpec((BLOCK, kvw), lambda b, i: (prev_block(b, i), 0)),
                  pl.BlockSpec((kvw, tq), lambda b, i: (0, b * tiles + i)),
                  pl.BlockSpec((kvw, BLOCK), lambda b, i: (0, prev_block(b, i)))] + [slab(w) for w in to_bf16],
        out_specs=[pl.BlockSpec((tq, attn_w), lambda b, i: (b * tiles + i, 0))] + [slab(w) for w in to_bf16],
        out_shape=[jax.ShapeDtypeStruct((t, attn_w), BF16)] + [jax.ShapeDtypeStruct(w.shape, BF16) for w in to_bf16],
        scratch_shapes=[pltpu.VMEM((tq + BLOCK, kvw), BF16), pltpu.VMEM((kvw, tq + BLOCK), BF16)],
        compiler_params=pltpu.CompilerParams(dimension_semantics=("arbitrary", "arbitrary"),
                                             vmem_limit_bytes=VMEM_LIMIT_BYTES),
        name="swa_attention",
    )(sinks, qt, k, k, vt, vt, *to_bf16)
    return outs[0], outs[1:]


def _conv_kernel(uc_ref, up_ref, wdw_ref, bdw_ref, lng_ref, lnb_ref, wpw_ref, bpw_ref, o_ref, head_ref, y_ref, *, tc):
    i = pl.program_id(1)
    n_slabs = uc_ref.shape[0]
    unit = SUBLANES * CONV_STRIDE
    assert unit == CONV_HALO
    halo = up_ref[...]
    head_ref[:, :CONV_HALO, :] = jnp.where(i > 0, halo, jnp.zeros_like(halo))
    head_ref[:, CONV_HALO:, :] = uc_ref[:, :unit, :]

    shift0 = CONV_HALO - (CONV_KERNEL - 1)
    passes = ((0, (CONV_KERNEL + 1) // 2), ((CONV_KERNEL + 1) // 2, CONV_KERNEL))
    for c in range(n_slabs):
        for first, (j0, j1) in zip((True, False), passes):
            w_rows = {j: wdw_ref[SUBLANES * j:SUBLANES * (j + 1), c * LANES:(c + 1) * LANES] for j in range(j0, j1)}
            bias = jnp.broadcast_to(bdw_ref[:, c * LANES:(c + 1) * LANES], (SUBLANES, LANES))

            def unit_taps(src_ref, src_base, base, c=c, first=first, j0=j0, j1=j1, w_rows=w_rows, bias=bias):
                out = [y_ref.at[c, pl.ds(base + k, SUBLANES, stride=CONV_STRIDE), :] for k in range(CONV_STRIDE)]
                acc = [[bias if first else out[k][...], None] for k in range(CONV_STRIDE)]
                for jk in range(j0, j1 + CONV_STRIDE - 1):
                    v = src_ref[c, pl.ds(src_base + shift0 + jk, SUBLANES, stride=CONV_STRIDE), :]
                    for k in range(CONV_STRIDE):
                        j = jk - k
                        if j0 <= j < j1:
                            term = v * w_rows[j]
                            acc[k][j % 2] = term if acc[k][j % 2] is None else acc[k][j % 2] + term
                for k in range(CONV_STRIDE):
                    out[k][...] = acc[k][0] + acc[k][1]

            unit_taps(head_ref, 0, 0)

            def unit_body(r, carry, unit_taps=unit_taps):
                unit_taps(uc_ref, r * unit - CONV_HALO, r * unit)
                return carry

            lax.fori_loop(1, tc // unit, unit_body, 0, unroll=4)

    def activations(r0):
        y = jnp.concatenate([y_ref[c, r0:r0 + CONV_TAIL_ROWS, :] for c in range(n_slabs)], axis=1)
        return jax.nn.silu(_layer_norm(y, lng_ref[...], lnb_ref[...])).astype(BF16)

    act = activations(0)
    for r0 in range(0, tc, CONV_TAIL_ROWS):
        out = jnp.dot(act, wpw_ref[...], preferred_element_type=F32) + bpw_ref[...]
        if r0 + CONV_TAIL_ROWS < tc:
            act = activations(r0 + CONV_TAIL_ROWS)
        o_ref[r0:r0 + CONV_TAIL_ROWS, :] = out.astype(BF16)


def _conv_branch(u, wdw, bdw, lng, lnb, wpw, bpw, *, batch, seq, tc):
    n_slabs, t, _ = u.shape
    width = n_slabs * LANES
    tiles = seq // tc
    cur = pl.BlockSpec((n_slabs, tc, LANES), lambda b, i: (0, b * tiles + i, 0))
    prev = pl.BlockSpec((n_slabs, CONV_HALO, LANES),
                        lambda b, i: (0, b * (seq // CONV_HALO) + jnp.maximum(i * (tc // CONV_HALO) - 1, 0), 0))
    return pl.pallas_call(
        functools.partial(_conv_kernel, tc=tc),
        grid=(batch, tiles),
        in_specs=[cur, prev, _resident(wdw.shape), _resident(bdw.shape), _resident(lng.shape), _resident(lnb.shape),
                  _resident(wpw.shape), _resident(bpw.shape)],
        out_specs=pl.BlockSpec((tc, width), lambda b, i: (b * tiles + i, 0)),
        out_shape=jax.ShapeDtypeStruct((t, width), BF16),
        scratch_shapes=[pltpu.VMEM((n_slabs, 2 * CONV_HALO, LANES), F32), pltpu.VMEM((n_slabs, tc, LANES), F32)],
        compiler_params=pltpu.CompilerParams(dimension_semantics=("arbitrary", "arbitrary"),
                                             vmem_limit_bytes=VMEM_LIMIT_BYTES),
        name="conformer_conv",
    )(u, u, wdw, bdw, lng, lnb, wpw, bpw)


def _out_proj_kernel(a_ref, c_ref, x_ref, w_ref, b_ref, g_ref, beta_ref, o_ref, *, alpha, splits):
    attn_w = a_ref.shape[1]
    rows = a_ref.shape[0] // splits

    def matmuls(r0):
        mix = jnp.dot(a_ref[r0:r0 + rows, :], w_ref[:attn_w, :], preferred_element_type=F32)
        return mix + jnp.dot(c_ref[r0:r0 + rows, :], w_ref[attn_w:, :], preferred_element_type=F32)

    def norm(r0, mix):
        y = alpha * x_ref[r0:r0 + rows, :] + (mix + b_ref[...])
        o_ref[r0:r0 + rows, :] = _layer_norm(y, g_ref[...], beta_ref[...])

    mix = matmuls(0)
    for k in range(splits):
        nxt = matmuls((k + 1) * rows) if k + 1 < splits else None
        norm(k * rows, mix)
        mix = nxt


def _out_proj(attn, conv, x2, w, b, g, beta, *, alpha, tm, splits):
    t, d = x2.shape
    row = lambda width: pl.BlockSpec((tm, width), lambda i: (i, 0))
    return pl.pallas_call(
        functools.partial(_out_proj_kernel, alpha=alpha, splits=splits),
        grid=(t // tm,),
        in_specs=[row(attn.shape[1]), row(conv.shape[1]), row(d), _resident(w.shape), _resident(b.shape),
                  _resident(g.shape), _resident(beta.shape)],
        out_specs=row(d),
        out_shape=jax.ShapeDtypeStruct((t, d), F32),
        compiler_params=pltpu.CompilerParams(dimension_semantics=("arbitrary",), vmem_limit_bytes=VMEM_LIMIT_BYTES),
        name="out_proj_ln",
    )(attn, conv, x2, w, b, g, beta)


def _ffn_kernel(x_ref, wg_hbm, wu_hbm, wd_hbm, g_ref, beta_ref, o_ref, wg_buf, wu_buf, wd_buf, sems, *, alpha, tf):
    i = pl.program_id(0)
    n_tiles = pl.num_programs(0)
    nf = wg_hbm.shape[1] // tf

    def copies(f, slot):
        cols = pl.ds(pl.multiple_of(f * tf, tf), tf)
        return (pltpu.make_async_copy(wg_hbm.at[:, cols], wg_buf.at[slot], sems.at[0, slot]),
                pltpu.make_async_copy(wu_hbm.at[:, cols], wu_buf.at[slot], sems.at[1, slot]),
                pltpu.make_async_copy(wd_hbm.at[cols, :], wd_buf.at[slot], sems.at[2, slot]))

    @pl.when(i == 0)
    def _():
        for c in copies(0, 0):
            c.start()

    o_ref[...] = alpha * x_ref[...]

    def chunk(f, carry):
        c_no = i * nf + f
        slot = c_no % 2
        for c in copies(f, slot):
            c.wait()

        @pl.when(c_no + 1 < n_tiles * nf)
        def _():
            for c in copies((f + 1) % nf, 1 - slot):
                c.start()

        xb = x_ref[...].astype(BF16)
        hidden = []
        for c0 in range(0, tf, tf // 2):
            gate = jnp.dot(xb, wg_buf[slot, :, c0:c0 + tf // 2], preferred_element_type=F32)
            up = jnp.dot(xb, wu_buf[slot, :, c0:c0 + tf // 2], preferred_element_type=F32)
            hidden.append((jax.nn.silu(gate) * up).astype(BF16))
        for h, c0 in zip(hidden, range(0, tf, tf // 2)):
            o_ref[...] += jnp.dot(h, wd_buf[slot, c0:c0 + tf // 2, :], preferred_element_type=F32)
        return carry

    lax.fori_loop(0, nf, chunk, 0)
    o_ref[...] = _layer_norm(o_ref[...], g_ref[...], beta_ref[...])


def _ffn(x1, wg, wu, wd, g, beta, *, alpha, tm, tf):
    t, d = x1.shape
    assert wg.shape[1] % tf == 0
    hbm = pl.BlockSpec(memory_space=pl.ANY)
    return pl.pallas_call(
        functools.partial(_ffn_kernel, alpha=alpha, tf=tf),
        grid=(t // tm,),
        in_specs=[pl.BlockSpec((tm, d), lambda i: (i, 0)), hbm, hbm, hbm, _resident(g.shape), _resident(beta.shape)],
        out_specs=pl.BlockSpec((tm, d), lambda i: (i, 0)),
        out_shape=jax.ShapeDtypeStruct((t, d), F32),
        scratch_shapes=[pltpu.VMEM((2, d, tf), BF16), pltpu.VMEM((2, d, tf), BF16), pltpu.VMEM((2, tf, d), BF16),
                        pltpu.SemaphoreType.DMA((3, 2))],
        compiler_params=pltpu.CompilerParams(dimension_semantics=("arbitrary",), vmem_limit_bytes=VMEM_LIMIT_BYTES),
        name="swiglu_ffn_ln",
    )(x1, wg, wu, wd, g, beta)


def kernel(x, positions, w_in, b_in, sinks, w_dw, b_dw, conv_ln_g, conv_ln_b, w_pw2, b_pw2, w_out, b_out, ln1_g,
           ln1_b, w_gate, w_up, w_down, ln2_g, ln2_b):
    batch, seq, d = x.shape
    depth = w_in.shape[0]
    alpha = (2 * depth) ** 0.25
    t = batch * seq
    row = lambda a: a.reshape(1, -1).astype(F32)

    half = HEAD_DIM // 2
    inv_freq = 1.0 / (ROPE_THETA ** (jnp.arange(half, dtype=F32) * 2.0 / HEAD_DIM))
    invf_col = inv_freq.reshape(half, 1)
    pos_row = positions.reshape(1, t)
    n_qkv = (N_Q_HEADS + 2 * N_KV_HEADS) * HEAD_DIM

    x2 = x.reshape(t, d)
    for l in range(depth):
        qt, k, vt, u = _in_proj(x2, pos_row, invf_col, w_in[l][:, :n_qkv].T.astype(BF16),
                                b_in[l][:n_qkv].reshape(-1, 1).astype(F32), w_in[l][:, n_qkv:].astype(BF16),
                                row(b_in[l][n_qkv:]), tm=TOKEN_TILE)
        later_weights = [w.astype(F32) for w in (w_pw2[l], w_out[l], w_gate[l], w_up[l], w_down[l])]
        attn, (wpw, wout, wg, wu, wd) = _attention(sinks[l].astype(F32), qt, k, vt, later_weights, batch=batch,
                                                   seq=seq, tq=ATTN_TILE)
        wdw = jnp.repeat(w_dw[l].reshape(CONV_KERNEL, -1).astype(F32), SUBLANES, axis=0)
        conv = _conv_branch(u, wdw, row(b_dw[l]), row(conv_ln_g[l]), row(conv_ln_b[l]), wpw, row(b_pw2[l]),
                            batch=batch, seq=seq, tc=CONV_TILE)
        x2 = _out_proj(attn, conv, x2, wout, row(b_out[l]), row(ln1_g[l]), row(ln1_b[l]), alpha=alpha,
                       tm=TOKEN_TILE, splits=4)
        x2 = _ffn(x2, wg, wu, wd, row(ln2_g[l]), row(ln2_b[l]), alpha=alpha, tm=TOKEN_TILE, tf=FFN_CHUNK)
    return x2.reshape(batch, seq, d)
```
